```python
import jax, jax.numpy as jnp
from jax import lax
import numpy as np

D_MODEL = 2048
BATCH = 4
SEQ = 4096
DEPTH = 1

HEAD_DIM = 128
ROPE_THETA = 10000.0
BLOCK = 128
A_PATTERNS = ((128, 1), (512, 4), (2048, 16))
A_HEADS_PER_GROUP = 4
A_GROUPS = len(A_PATTERNS)
A_HEADS = A_HEADS_PER_GROUP * A_GROUPS
A_QKV = A_HEADS * HEAD_DIM
A_OUT = A_HEADS_PER_GROUP * HEAD_DIM
B_Q_HEADS = 8
B_KV_HEADS = 2
B_GROUP = B_Q_HEADS // B_KV_HEADS
B_WINDOW = 128
B_Q = B_Q_HEADS * HEAD_DIM
B_KV = B_KV_HEADS * HEAD_DIM
IN_COLS = 3 * A_QKV + B_Q + 2 * B_KV + 2 * D_MODEL
PEER_HEADS = 8
PEER_QDIM = 256
PEER_HALF = PEER_QDIM // 2
N_KEYS = 128
N_EXPERTS = N_KEYS * N_KEYS
PEER_TOPK = 16
PEER_CHUNK = 128
PLE_DIM = 256
LN_EPS = 1e-5
DN_ALPHA = (2 * DEPTH) ** 0.25
DN_BETA = (8 * DEPTH) ** -0.25

kernel_name = "hybrid_dilated_swa_peer_deepnorm"


def layer_norm(x, g, b):
    xf = x.astype(jnp.float32)
    mu = jnp.mean(xf, axis=-1, keepdims=True)
    var = jnp.mean(jnp.square(xf - mu), axis=-1, keepdims=True)
    return ((xf - mu) * lax.rsqrt(var + LN_EPS) * g.astype(jnp.float32) + b.astype(jnp.float32)).astype(x.dtype)


def rope_tables(seq):
    inv = 1.0 / (ROPE_THETA ** (jnp.arange(0, HEAD_DIM, 2, dtype=jnp.float32) / HEAD_DIM))
    ang = jnp.arange(seq, dtype=jnp.float32)[:, None] * inv[None, :]
    return jnp.cos(ang), jnp.sin(ang)


def apply_rope(t, cos, sin):
    t1, t2 = jnp.split(t.astype(jnp.float32), 2, axis=-1)
    c = cos[:, None, :]
    s = sin[:, None, :]
    return jnp.concatenate([t1 * c - t2 * s, t1 * s + t2 * c], axis=-1).astype(t.dtype)


def banded_causal_attention(q, k, v, max_dist, sink=None):
    L, hd = q.shape[-2], q.shape[-1]
    nb = -(-L // BLOCK)
    pad = nb * BLOCK - L
    if pad:
        q = jnp.pad(q, [(0, 0)] * (q.ndim - 2) + [(0, pad), (0, 0)])
        k = jnp.pad(k, [(0, 0)] * (k.ndim - 2) + [(0, pad), (0, 0)])
        v = jnp.pad(v, [(0, 0)] * (v.ndim - 2) + [(0, pad), (0, 0)])
    lead = k.shape[:-2]
    qb = q.reshape(*q.shape[:-2], nb, BLOCK, hd)

    def band(t):
        tb = t.reshape(*lead, nb, BLOCK, hd)
        prev = jnp.concatenate([jnp.zeros_like(tb[..., :1, :, :]), tb[..., :-1, :, :]], axis=-3)
        return jnp.concatenate([prev, tb], axis=-2)

    kb, vb = band(k), band(v)
    s = jnp.einsum('...gnqd,...nkd->...gnqk', qb, kb,
                   preferred_element_type=jnp.float32) * (hd ** -0.5)
    qpos = jnp.arange(nb)[:, None, None] * BLOCK + jnp.arange(BLOCK)[None, :, None]
    kpos = jnp.arange(nb)[:, None, None] * BLOCK - BLOCK + jnp.arange(2 * BLOCK)[None, None, :]
    dist = qpos - kpos
    mask = (dist >= 0) & (dist <= max_dist) & (kpos >= 0)
    s = jnp.where(mask, s, -jnp.inf)
    m = jnp.max(s, axis=-1)
    if sink is not None:
        m = jnp.maximum(m, sink)
    pr = jnp.exp(s - m[..., None])
    l = jnp.sum(pr, axis=-1)
    denom = l + jnp.exp(sink - m) if sink is not None else l
    o = jnp.einsum('...gnqk,...nkd->...gnqd', pr, vb.astype(jnp.float32)) / denom[..., None]
    o = o.reshape(*o.shape[:-3], nb * BLOCK, hd)[..., :L, :]
    m = m.reshape(*m.shape[:-2], nb * BLOCK)[..., :L]
    l = l.reshape(*l.shape[:-2], nb * BLOCK)[..., :L]
    return o, m, l


def dilated_group_attention(q, k, v, window, dil):
    Bn, H, S, hd = q.shape
    L = S // dil

    def gather(t):
        return t.reshape(Bn, H, L, dil, hd).transpose(0, 1, 3, 2, 4)

    o, m, l = banded_causal_attention(gather(q)[..., None, :, :], gather(k), gather(v), window // dil)
    o = o[..., 0, :, :].transpose(0, 1, 3, 2, 4).reshape(Bn, H, S, hd)
    m = m[..., 0, :].transpose(0, 1, 3, 2).reshape(Bn, H, S)
    l = l[..., 0, :].transpose(0, 1, 3, 2).reshape(Bn, H, S)
    return o, m, l


def hybrid_mixer(h, cos, sin, w_in, sinks, w_branch_a, w_branch_b, w_out):
    Bn, S, _ = h.shape
    z = h @ w_in
    cuts = [int(c) for c in np.cumsum([A_QKV, A_QKV, A_QKV, B_Q, B_KV, B_KV, D_MODEL])]
    qa, ka, va, qb, kb, vb, ga, gb = jnp.split(z, cuts, axis=-1)

    def heads_a(t):
        t = t.reshape(Bn, S, A_GROUPS, A_HEADS_PER_GROUP, HEAD_DIM)
        return t.transpose(2, 0, 3, 1, 4)
    qa = heads_a(apply_rope(qa.reshape(Bn, S, A_HEADS, HEAD_DIM), cos, sin))
    ka = heads_a(apply_rope(ka.reshape(Bn, S, A_HEADS, HEAD_DIM), cos, sin))
    va = heads_a(va)
    os_, ms_, ls_ = [], [], []
    for g, (window, dil) in enumerate(A_PATTERNS):
        o, m, l = dilated_group_attention(qa[g], ka[g], va[g], window, dil)
        os_.append(o)
        ms_.append(m)
        ls_.append(l)
    os_, ms_, ls_ = jnp.stack(os_), jnp.stack(ms_), jnp.stack(ls_)
    wts = ls_ * jnp.exp(ms_ - jnp.max(ms_, axis=0, keepdims=True))
    o_a = jnp.sum(wts[..., None] * os_, axis=0) / jnp.sum(wts, axis=0)[..., None]
    o_a = o_a.transpose(0, 2, 1, 3).reshape(Bn, S, A_OUT).astype(h.dtype)

    qb = apply_rope(qb.reshape(Bn, S, B_Q_HEADS, HEAD_DIM), cos, sin)
    qb = qb.reshape(Bn, S, B_KV_HEADS, B_GROUP, HEAD_DIM).transpose(0, 2, 3, 1, 4)
    kb = apply_rope(kb.reshape(Bn, S, B_KV_HEADS, HEAD_DIM), cos, sin).transpose(0, 2, 1, 3)
    vb = vb.reshape(Bn, S, B_KV_HEADS, HEAD_DIM).transpose(0, 2, 1, 3)
    sink = sinks.astype(jnp.float32).reshape(B_KV_HEADS, B_GROUP, 1, 1)
    o_b, _, _ = banded_causal_attention(qb, kb, vb, B_WINDOW - 1, sink)
    o_b = o_b.transpose(0, 3, 1, 2, 4).reshape(Bn, S, B_Q).astype(h.dtype)

    merged = jax.nn.sigmoid(ga) * (o_a @ w_branch_a) + jax.nn.sigmoid(gb) * (o_b @ w_branch_b)
    return merged @ w_out


def peer_ffn(h, wq, subkeys, u_tab, v_tab):
    Bn, S, D = h.shape
    T = Bn * S
    xt = h.reshape(T, D)
    q = (xt @ wq).reshape(T, PEER_HEADS, 2, PEER_HALF)
    s = jnp.einsum('thcd,hcnd->thcn', q, subkeys, preferred_element_type=jnp.float32)
    top_s, top_i = lax.top_k(s, PEER_TOPK)
    cand = top_s[:, :, 0, :, None] + top_s[:, :, 1, None, :]
    best_s, best_c = lax.top_k(cand.reshape(T, PEER_HEADS, PEER_TOPK * PEER_TOPK), PEER_TOPK)
    i1 = jnp.take_along_axis(top_i[:, :, 0], best_c // PEER_TOPK, axis=-1)
    i2 = jnp.take_along_axis(top_i[:, :, 1], best_c % PEER_TOPK, axis=-1)
    expert = i1 * N_KEYS + i2
    gate = jax.nn.softmax(best_s, axis=-1)
    nc = T // PEER_CHUNK

    def chunk(args):
        xc, ec, gc = args
        u = u_tab[ec]
        act = jax.nn.gelu(jnp.einsum('cd,chkd->chk', xc, u, preferred_element_type=jnp.float32),
                          approximate=False)
        w = (gc * act).astype(v_tab.dtype)
        return jnp.einsum('chk,chkd->cd', w, v_tab[ec])

    out = lax.map(chunk, (xt.reshape(nc, PEER_CHUNK, D),
                          expert.reshape(nc, PEER_CHUNK, PEER_HEADS, PEER_TOPK),
                          gate.reshape(nc, PEER_CHUNK, PEER_HEADS, PEER_TOPK)))
    return out.reshape(Bn, S, D).astype(h.dtype)


def setup_inputs(seed: int = 0) -> dict:
    key = jax.random.key(seed)
    ks = jax.random.split(key, 18)
    f = jnp.float32

    def nrm(k, shape, scale):
        return jax.random.normal(k, shape, f) * scale

    return {
        "x": nrm(ks[0], (BATCH, SEQ, D_MODEL), 1.0),
        "p": nrm(ks[1], (DEPTH, BATCH, SEQ, PLE_DIM), 1.0),
        "w_in": nrm(ks[2], (DEPTH, D_MODEL, IN_COLS), D_MODEL ** -0.5),
        "sinks": nrm(ks[3], (DEPTH, B_Q_HEADS), 0.5),
        "w_branch_a": nrm(ks[4], (DEPTH, A_OUT, D_MODEL), A_OUT ** -0.5),
        "w_branch_b": nrm(ks[5], (DEPTH, B_Q, D_MODEL), B_Q ** -0.5),
        "w_out": nrm(ks[6], (DEPTH, D_MODEL, D_MODEL), DN_BETA * D_MODEL ** -0.5),
        "ln1_g": 1.0 + nrm(ks[7], (DEPTH, D_MODEL), 0.02),
        "ln1_b": nrm(ks[8], (DEPTH, D_MODEL), 0.02),
        "peer_wq": nrm(ks[9], (DEPTH, D_MODEL, PEER_HEADS * PEER_QDIM), D_MODEL ** -0.5),
        "peer_subkeys": nrm(ks[10], (DEPTH, PEER_HEADS, 2, N_KEYS, PEER_HALF), PEER_HALF ** -0.5),
        "peer_u": nrm(ks[11], (DEPTH, N_EXPERTS, D_MODEL), D_MODEL ** -0.5),
        "peer_v": nrm(ks[12], (DEPTH, N_EXPERTS, D_MODEL), DN_BETA * PEER_HEADS ** -0.5),
        "ple_gate": nrm(ks[13], (DEPTH, D_MODEL, D_MODEL), D_MODEL ** -0.5),
        "ple_proj": nrm(ks[14], (DEPTH, PLE_DIM, D_MODEL), DN_BETA * PLE_DIM ** -0.5),
        "ln2_g": 1.0 + nrm(ks[15], (DEPTH, D_MODEL), 0.02),
        "ln2_b": nrm(ks[16], (DEPTH, D_MODEL), 0.02),
    }


def reference(x, p, w_in, sinks, w_branch_a, w_branch_b, w_out, ln1_g, ln1_b,
              peer_wq, peer_subkeys, peer_u, peer_v, ple_gate, ple_proj, ln2_g, ln2_b):
    cos, sin = rope_tables(x.shape[1])
    for i in range(DEPTH):
        mix = hybrid_mixer(x, cos, sin, w_in[i], sinks[i], w_branch_a[i], w_branch_b[i], w_out[i])
        h1 = layer_norm(DN_ALPHA * x + mix, ln1_g[i], ln1_b[i])
        ffn = peer_ffn(h1, peer_wq[i], peer_subkeys[i], peer_u[i], peer_v[i])
        ple = jax.nn.sigmoid(h1 @ ple_gate[i]) * (p[i] @ ple_proj[i])
        x = layer_norm(DN_ALPHA * h1 + ffn + ple, ln2_g[i], ln2_b[i])
    return x
```

```python
import functools

import numpy as np
import jax
import jax.numpy as jnp
from jax import lax
from jax.experimental import pallas as pl
from jax.experimental.pallas import tpu as pltpu

HEAD_DIM = 128
ROPE_THETA = 10000.0
BLOCK = 128
A_PATTERNS = ((128, 1), (512, 4), (2048, 16))
A_HEADS_PER_GROUP = 4
A_GROUPS = len(A_PATTERNS)
A_HEADS = A_HEADS_PER_GROUP * A_GROUPS
A_QKV = A_HEADS * HEAD_DIM
A_OUT = A_HEADS_PER_GROUP * HEAD_DIM
B_Q_HEADS = 8
B_KV_HEADS = 2
B_GROUP = B_Q_HEADS // B_KV_HEADS
B_Q = B_Q_HEADS * HEAD_DIM
B_KV = B_KV_HEADS * HEAD_DIM
PEER_HEADS = 8
PEER_HALF = 128
N_KEYS = 128
PEER_TOPK = 16
LN_EPS = 1e-5

V7X_VMEM_LIMIT = 56 * 1024 * 1024
A_TQ = max(d for _, d in A_PATTERNS) * BLOCK

_NT = (((1,), (1,)), ((), ()))
_BF = jnp.bfloat16
_F32 = jnp.float32


def _cparams(sem):
    return pltpu.CompilerParams(dimension_semantics=sem, vmem_limit_bytes=V7X_VMEM_LIMIT)


def _resident(block_shape, index_map):
    return pl.BlockSpec(block_shape, index_map, pipeline_mode=pl.Buffered(1))


def _inproj_kernel(x_ref, w_ref, cos_ref, sin_ref, o_ref, *, patterns):
    j = pl.program_id(1)
    acc = jnp.dot(x_ref[...], w_ref[...], preferred_element_type=_F32)
    tn = acc.shape[1]
    groups = {}
    for jb, pat in enumerate(patterns):
        groups.setdefault(pat, []).append(jb)
    for pat, blocks in groups.items():
        cond = functools.reduce(jnp.logical_or, [j == jb for jb in blocks])

        @pl.when(cond)
        def _(pat=pat):
            if not any(pat):
                o_ref[...] = acc
                return
            for c in range(tn // HEAD_DIM):
                t = acc[:, c * HEAD_DIM:(c + 1) * HEAD_DIM]
                if pat[c]:
                    t = t * cos_ref[...] + pltpu.roll(t, HEAD_DIM // 2, 1) * sin_ref[...]
                o_ref[:, c * HEAD_DIM:(c + 1) * HEAD_DIM] = t


def _inproj(xb, wb, cos2, sin2, seq, rope_cols, tm=1024, tn=512):
    m, k = xb.shape
    n = wb.shape[1]
    tm = min(tm, seq)
    patterns = tuple(
        tuple(any(lo <= (jb * tn + c * HEAD_DIM) < hi for lo, hi in rope_cols) for c in range(tn // HEAD_DIM))
        for jb in range(n // tn))
    sblocks = seq // tm
    return pl.pallas_call(
        functools.partial(_inproj_kernel, patterns=patterns),
        grid=(m // tm, n // tn),
        in_specs=[
            pl.BlockSpec((tm, k), lambda i, j: (i, 0)),
            pl.BlockSpec((k, tn), lambda i, j: (0, j)),
            pl.BlockSpec((tm, HEAD_DIM), lambda i, j: (i % sblocks, 0)),
            pl.BlockSpec((tm, HEAD_DIM), lambda i, j: (i % sblocks, 0)),
        ],
        out_specs=pl.BlockSpec((tm, tn), lambda i, j: (i, j)),
        out_shape=jax.ShapeDtypeStruct((m, n), _F32),
        compiler_params=_cparams(("parallel", "arbitrary")),
        name="inproj_rope",
    )(xb, wb, cos2, sin2)


def _band_scores(q, kc, kp, has_prev, strict_prev):
    scale = HEAD_DIM ** -0.5
    sc = lax.dot_general(q, kc, _NT, preferred_element_type=_F32) * scale
    sp = lax.dot_general(q, kp, _NT, preferred_element_type=_F32) * scale
    row = lax.broadcasted_iota(jnp.int32, sc.shape, 0) % BLOCK
    col = lax.broadcasted_iota(jnp.int32, sc.shape, 1)
    prev_ok = (col > row) if strict_prev else (col >= row)
    sc = jnp.where(col <= row, sc, -jnp.inf)
    sp = jnp.where(jnp.logical_and(prev_ok, has_prev), sp, -jnp.inf)
    return sc, sp


def _attn_a_kernel(q0, q1, q2, k0, k1, k2, v0, v1, v2, o_ref, acc_s, m_s, l_s):
    tq = o_ref.shape[0]
    t0 = pl.program_id(2) * tq
    refs = ((q0, k0, v0), (q1, k1, v1), (q2, k2, v2))
    for g, (window, dil) in enumerate(A_PATTERNS):
        q_ref, k_ref, v_ref = refs[g]
        span = BLOCK * dil

        def unit(u, carry, q_ref=q_ref, k_ref=k_ref, v_ref=v_ref, dil=dil, span=span, g=g):
            sb = u // dil
            r = u % dil
            loc = sb * span + r
            has_prev = (t0 + sb * span) > 0
            pstart = jnp.maximum(t0 + loc - span, r)
            q = q_ref[pl.ds(loc, BLOCK, stride=dil), :].astype(_BF)
            kc = k_ref[pl.ds(t0 + loc, BLOCK, stride=dil), :].astype(_BF)
            vc = v_ref[pl.ds(t0 + loc, BLOCK, stride=dil), :].astype(_BF)
            kp = k_ref[pl.ds(pstart, BLOCK, stride=dil), :].astype(_BF)
            vp = v_ref[pl.ds(pstart, BLOCK, stride=dil), :].astype(_BF)
            sc, sp = _band_scores(q, kc, kp, has_prev, strict_prev=False)
            m = jnp.maximum(jnp.max(sc, axis=1, keepdims=True), jnp.max(sp, axis=1, keepdims=True))
            pc = jnp.exp(sc - m)
            pp = jnp.exp(sp - m)
            l = jnp.sum(pc, axis=1, keepdims=True) + jnp.sum(pp, axis=1, keepdims=True)
            acc = (jnp.dot(pc.astype(_BF), vc, preferred_element_type=_F32)
                   + jnp.dot(pp.astype(_BF), vp, preferred_element_type=_F32))
            rows = pl.ds(loc, BLOCK, stride=dil)
            acc_s[g, rows, :] = acc
            m_s[g, rows, :] = jnp.broadcast_to(m, acc.shape)
            l_s[g, rows, :] = jnp.broadcast_to(l, acc.shape)
            return carry

        lax.fori_loop(0, tq // BLOCK, unit, 0)

    chunk = 256

    def combine(c, carry):
        rows = pl.ds(pl.multiple_of(c * chunk, chunk), chunk)
        ms = [m_s[g, rows, :] for g in range(A_GROUPS)]
        mx = functools.reduce(jnp.maximum, ms)
        ws = [jnp.exp(m - mx) for m in ms]
        num = sum(w * acc_s[g, rows, :] for g, w in enumerate(ws))
        den = sum(w * l_s[g, rows, :] for g, w in enumerate(ws))
        o_ref[rows, :] = (num / den).astype(o_ref.dtype)
        return carry

    lax.fori_loop(0, tq // chunk, combine, 0)


def _attn_a(z, bsz, seq):
    tq = A_TQ
    nh = A_HEADS_PER_GROUP
    kcol = A_QKV // HEAD_DIM
    vcol = 2 * A_QKV // HEAD_DIM

    def qspec(g):
        return pl.BlockSpec((None, tq, HEAD_DIM), lambda b, j, t, g=g: (b, t, g * nh + j))

    def kvspec(g, base):
        return pl.BlockSpec((None, seq, HEAD_DIM), lambda b, j, t, g=g, base=base: (b, 0, base + g * nh + j))

    in_specs = ([qspec(g) for g in range(A_GROUPS)] + [kvspec(g, kcol) for g in range(A_GROUPS)]
                + [kvspec(g, vcol) for g in range(A_GROUPS)])
    return pl.pallas_call(
        _attn_a_kernel,
        grid=(bsz, nh, seq // tq),
        in_specs=in_specs,
        out_specs=pl.BlockSpec((None, tq, HEAD_DIM), lambda b, j, t: (b, t, j)),
        out_shape=jax.ShapeDtypeStruct((bsz, seq, A_OUT), _BF),
        scratch_shapes=[pltpu.VMEM((A_GROUPS, tq, HEAD_DIM), _F32)] * 3,
        compiler_params=_cparams(("parallel", "parallel", "arbitrary")),
        name="attn_dilated",
    )(*([z] * 9))


def _attn_b_kernel(sink_ref, q_ref, k_ref, v_ref, o_ref):
    tq = o_ref.shape[0]
    t0 = pl.program_id(2) * tq
    c = pl.program_id(1)
    sink = jnp.concatenate(
        [jnp.full((BLOCK, 1), sink_ref[c * B_GROUP + g], _F32) for g in range(B_GROUP)], axis=0)

    def unit(u, carry):
        loc = pl.multiple_of(u * BLOCK, BLOCK)
        has_prev = (t0 + loc) > 0
        pstart = pl.multiple_of(jnp.maximum(t0 + loc - BLOCK, 0), BLOCK)
        q4 = q_ref[pl.ds(loc, BLOCK), :].astype(_BF)
        q = jnp.concatenate([q4[:, g * HEAD_DIM:(g + 1) * HEAD_DIM] for g in range(B_GROUP)], axis=0)
        kc = k_ref[pl.ds(pl.multiple_of(t0 + loc, BLOCK), BLOCK), :].astype(_BF)
        vc = v_ref[pl.ds(pl.multiple_of(t0 + loc, BLOCK), BLOCK), :].astype(_BF)
        kp = k_ref[pl.ds(pstart, BLOCK), :].astype(_BF)
        vp = v_ref[pl.ds(pstart, BLOCK), :].astype(_BF)
        sc, sp = _band_scores(q, kc, kp, has_prev, strict_prev=True)
        m = jnp.maximum(jnp.max(sc, axis=1, keepdims=True), jnp.max(sp, axis=1, keepdims=True))
        m = jnp.maximum(m, sink)
        pc = jnp.exp(sc - m)
        pp = jnp.exp(sp - m)
        denom = jnp.sum(pc, axis=1, keepdims=True) + jnp.sum(pp, axis=1, keepdims=True) + jnp.exp(sink - m)
        acc = (jnp.dot(pc.astype(_BF), vc, preferred_element_type=_F32)
               + jnp.dot(pp.astype(_BF), vp, preferred_element_type=_F32))
        o = (acc / denom).astype(o_ref.dtype)
        for g in range(B_GROUP):
            o_ref[pl.ds(loc, BLOCK), g * HEAD_DIM:(g + 1) * HEAD_DIM] = o[g * BLOCK:(g + 1) * BLOCK, :]
        return carry

    lax.fori_loop(0, tq // BLOCK, unit, 0)


def _attn_b(z, sinks, bsz, seq, tq=1024):
    tq = min(tq, seq)
    gq = B_GROUP * HEAD_DIM
    qcol = 3 * A_QKV // gq
    kcol = (3 * A_QKV + B_Q) // HEAD_DIM
    vcol = (3 * A_QKV + B_Q + B_KV) // HEAD_DIM
    return pl.pallas_call(
        _attn_b_kernel,
        grid=(bsz, B_KV_HEADS, seq // tq),
        in_specs=[
            pl.BlockSpec(memory_space=pltpu.SMEM),
            pl.BlockSpec((None, tq, gq), lambda b, c, t: (b, t, qcol + c)),
            pl.BlockSpec((None, seq, HEAD_DIM), lambda b, c, t: (b, 0, kcol + c)),
            pl.BlockSpec((None, seq, HEAD_DIM), lambda b, c, t: (b, 0, vcol + c)),
        ],
        out_specs=pl.BlockSpec((None, tq, gq), lambda b, c, t: (b, t, c)),
        out_shape=jax.ShapeDtypeStruct((bsz, seq, B_Q), _BF),
        compiler_params=_cparams(("parallel", "parallel", "arbitrary")),
        name="attn_swa_gqa",
    )(sinks, z, z, z)


def _layer_norm(h, g, b):
    mu = jnp.mean(h, axis=-1, keepdims=True)
    var = jnp.mean(jnp.square(h - mu), axis=-1, keepdims=True)
    return (h - mu) * lax.rsqrt(var + LN_EPS) * g + b


def _merge_kernel(oa_ref, ob_ref, ga_ref, gb_ref, x_ref, wa_ref, wb_ref, wo_ref, g_ref, b_ref,
                  h_ref, hb_ref, *, alpha):
    ma = jnp.dot(oa_ref[...], wa_ref[...], preferred_element_type=_F32)
    mb = jnp.dot(ob_ref[...], wb_ref[...], preferred_element_type=_F32)
    merged = jax.nn.sigmoid(ga_ref[...]) * ma + jax.nn.sigmoid(gb_ref[...]) * mb
    mix = jnp.dot(merged.astype(_BF), wo_ref[...], preferred_element_type=_F32)
    h = _layer_norm(alpha * x_ref[...] + mix, g_ref[...], b_ref[...])
    h_ref[...] = h
    hb_ref[...] = h.astype(_BF)


def _merge(oa, ob, z2, x2, wa, wb, wo, g, b, alpha, tm=256):
    t, d = x2.shape
    gcol = (3 * A_QKV + B_Q + 2 * B_KV) // d
    return pl.pallas_call(
        functools.partial(_merge_kernel, alpha=alpha),
        grid=(t // tm,),
        in_specs=[
            pl.BlockSpec((tm, A_OUT), lambda i: (i, 0)),
            pl.BlockSpec((tm, B_Q), lambda i: (i, 0)),
            pl.BlockSpec((tm, d), lambda i: (i, gcol)),
            pl.BlockSpec((tm, d), lambda i: (i, gcol + 1)),
            pl.BlockSpec((tm, d), lambda i: (i, 0)),
            _resident((A_OUT, d), lambda i: (0, 0)),
            _resident((B_Q, d), lambda i: (0, 0)),
            _resident((d, d), lambda i: (0, 0)),
            _resident((1, d), lambda i: (0, 0)),
            _resident((1, d), lambda i: (0, 0)),
        ],
        out_specs=[pl.BlockSpec((tm, d), lambda i: (i, 0)), pl.BlockSpec((tm, d), lambda i: (i, 0))],
        out_shape=[jax.ShapeDtypeStruct((t, d), _F32), jax.ShapeDtypeStruct((t, d), _BF)],
        compiler_params=_cparams(("parallel",)),
        name="merge_outproj_ln",
    )(oa, ob, z2, z2, x2, wa, wb, wo, g, b)


def _extract_top(s, k):
    n = s.shape[0]
    iota = lax.broadcasted_iota(jnp.int32, s.shape, 0)
    rank = jnp.full(s.shape, float(k), _F32)
    vals, idxs = [], []
    for kk in range(k):
        m = jnp.max(s, axis=0, keepdims=True)
        idx = jnp.min(jnp.where(s == m, iota, n), axis=0, keepdims=True)
        sel = iota == idx
        rank = jnp.where(sel, float(kk), rank)
        s = jnp.where(sel, -jnp.inf, s)
        vals.append(m)
        idxs.append(idx)
    return jnp.concatenate(vals, axis=0), jnp.concatenate(idxs, axis=0), rank


def _route_kernel(h_ref, wq_ref, sk_ref, p1_ref, jj_ref, e2_ref, r2_ref, q_s):
    kk = PEER_TOPK
    q = jnp.dot(h_ref[...], wq_ref[...], preferred_element_type=_F32)
    for hc in range(2 * PEER_HEADS):
        q_s[hc] = q[:, hc * PEER_HALF:(hc + 1) * PEER_HALF].astype(_BF)

    def head(h, carry):
        s1 = lax.dot_general(sk_ref[h, 0], q_s[2 * h], _NT, preferred_element_type=_F32)
        s2 = lax.dot_general(sk_ref[h, 1], q_s[2 * h + 1], _NT, preferred_element_type=_F32)
        a1, _, r1 = _extract_top(s1, kk)
        a2, _, r2 = _extract_top(s2, kk)
        cand = jnp.concatenate([a1[i:i + 1, :] + a2 for i in range(kk)], axis=0)
        best, cidx, _ = _extract_top(cand, kk)
        ci = cidx // kk
        z = jnp.sum(jnp.exp(best - best[0:1, :]), axis=0, keepdims=True)
        jj = jnp.zeros(s1.shape, _F32)
        for i in range(kk):
            cnt = jnp.sum((ci == i).astype(_F32), axis=0, keepdims=True)
            jj = jnp.where(r1 == float(i), cnt, jj)
        p1_ref[h] = jnp.exp(s1 - a1[0:1, :]) / z
        jj_ref[h] = jj
        e2_ref[h] = jnp.exp(s2 - a2[0:1, :])
        r2_ref[h] = r2
        return carry

    lax.fori_loop(0, PEER_HEADS, head, 0)


def _route(hb, wq, sk, tm=256):
    t, d = hb.shape
    out = jax.ShapeDtypeStruct((PEER_HEADS, N_KEYS, t), _F32)
    ospec = pl.BlockSpec((PEER_HEADS, N_KEYS, tm), lambda i: (0, 0, i))
    return pl.pallas_call(
        _route_kernel,
        grid=(t // tm,),
        in_specs=[
            pl.BlockSpec((tm, d), lambda i: (i, 0)),
            _resident(wq.shape, lambda i: (0, 0)),
            _resident(sk.shape, lambda i: (0, 0, 0, 0)),
        ],
        out_specs=[ospec] * 4,
        out_shape=[out] * 4,
        scratch_shapes=[pltpu.VMEM((2 * PEER_HEADS, tm, PEER_HALF), _BF)],
        compiler_params=_cparams(("parallel",)),
        name="peer_route",
    )(hb, wq, sk)


def _experts_kernel(h_ref, u_ref, vt_ref, p1_ref, jj_ref, e2_ref, r2_ref, o_ref, acc_s):
    e = pl.program_id(1)
    eb = u_ref.shape[0]
    nsub = eb // N_KEYS

    @pl.when(e == 0)
    def _():
        acc_s[...] = jnp.zeros_like(acc_s)

    act = lax.dot_general(u_ref[...], h_ref[...], _NT, preferred_element_type=_F32)
    ws = []
    for a in range(nsub):
        i1 = e * nsub + a
        gsum = None
        for h in range(PEER_HEADS):
            jrow = jj_ref[h, pl.ds(i1, 1), :]
            prow = p1_ref[h, pl.ds(i1, 1), :]
            term = jnp.where(r2_ref[h] < jrow, e2_ref[h], 0.0) * prow
            gsum = term if gsum is None else gsum + term
        x = act[a * N_KEYS:(a + 1) * N_KEYS, :]
        gelu = 0.5 * x * (1.0 + lax.erf(x * np.float32(np.sqrt(0.5))))
        ws.append((gsum * gelu).astype(_BF))
    w = jnp.concatenate(ws, axis=0)
    acc_s[...] += jnp.dot(vt_ref[...], w, preferred_element_type=_F32)

    @pl.when(e == pl.num_programs(1) - 1)
    def _():
        o_ref[...] = acc_s[...].T


def _experts(hb, ub, vtb, p1, jj, e2, r2, tm=512, eb=512):
    t, d = hb.shape
    ne = ub.shape[0]
    rspec = pl.BlockSpec((PEER_HEADS, N_KEYS, tm), lambda i, e: (0, 0, i))
    return pl.pallas_call(
        _experts_kernel,
        grid=(t // tm, ne // eb),
        in_specs=[
            pl.BlockSpec((tm, d), lambda i, e: (i, 0)),
            pl.BlockSpec((eb, d), lambda i, e: (e, 0)),
            pl.BlockSpec((d, eb), lambda i, e: (0, e)),
            rspec, rspec, rspec, rspec,
        ],
        out_specs=pl.BlockSpec((tm, d), lambda i, e: (i, 0)),
        out_shape=jax.ShapeDtypeStruct((t, d), _F32),
        scratch_shapes=[pltpu.VMEM((d, tm), _F32)],
        compiler_params=_cparams(("parallel", "arbitrary")),
        name="peer_experts",
    )(hb, ub, vtb, p1, jj, e2, r2)


def _final_kernel(h_ref, hb_ref, f_ref, p_ref, wg_ref, wp_ref, g_ref, b_ref, o_ref, *, alpha):
    gate = jax.nn.sigmoid(jnp.dot(hb_ref[...], wg_ref[...], preferred_element_type=_F32))
    ple = gate * jnp.dot(p_ref[...], wp_ref[...], preferred_element_type=_F32)
    o_ref[...] = _layer_norm(alpha * h_ref[...] + f_ref[...] + ple, g_ref[...], b_ref[...])


def _final(h, hb, ffn, pb, wg, wp, g, b, alpha, tm=256):
    t, d = h.shape
    pd = pb.shape[1]
    return pl.pallas_call(
        functools.partial(_final_kernel, alpha=alpha),
        grid=(t // tm,),
        in_specs=[
            pl.BlockSpec((tm, d), lambda i: (i, 0)),
            pl.BlockSpec((tm, d), lambda i: (i, 0)),
            pl.BlockSpec((tm, d), lambda i: (i, 0)),
            pl.BlockSpec((tm, pd), lambda i: (i, 0)),
            _resident((d, d), lambda i: (0, 0)),
            _resident((pd, d), lambda i: (0, 0)),
            _resident((1, d), lambda i: (0, 0)),
            _resident((1, d), lambda i: (0, 0)),
        ],
        out_specs=pl.BlockSpec((tm, d), lambda i: (i, 0)),
        out_shape=jax.ShapeDtypeStruct((t, d), _F32),
        compiler_params=_cparams(("parallel",)),
        name="ple_final_ln",
    )(h, hb, ffn, pb, wg, wp, g, b)


def _rope_tables(seq):
    inv = 1.0 / (ROPE_THETA ** (jnp.arange(0, HEAD_DIM, 2, dtype=_F32) / HEAD_DIM))
    ang = jnp.arange(seq, dtype=_F32)[:, None] * inv[None, :]
    cos, sin = jnp.cos(ang), jnp.sin(ang)
    return jnp.concatenate([cos, cos], axis=1), jnp.concatenate([-sin, sin], axis=1)


def kernel(x, p, w_in, sinks, w_branch_a, w_branch_b, w_out, ln1_g, ln1_b, peer_wq, peer_subkeys, peer_u,
           peer_v, ple_gate, ple_proj, ln2_g, ln2_b):
    bsz, seq, d = x.shape
    depth = w_in.shape[0]
    t = bsz * seq
    alpha = (2 * depth) ** 0.25
    assert seq % A_TQ == 0 and w_in.shape[2] == 3 * A_QKV + B_Q + 2 * B_KV + 2 * d
    cos2, sin2 = _rope_tables(seq)
    rope_cols = ((0, 2 * A_QKV), (3 * A_QKV, 3 * A_QKV + B_Q + B_KV))
    h = x.reshape(t, d)
    for i in range(depth):
        z = _inproj(h.astype(_BF), w_in[i].astype(_BF), cos2, sin2, seq, rope_cols)
        z3 = z.reshape(bsz, seq, -1)
        oa = _attn_a(z3, bsz, seq).reshape(t, A_OUT)
        ob = _attn_b(z3, sinks[i].astype(_F32), bsz, seq).reshape(t, B_Q)
        h1, h1b = _merge(oa, ob, z, h, w_branch_a[i].astype(_BF), w_branch_b[i].astype(_BF),
                         w_out[i].astype(_BF), ln1_g[i].reshape(1, d), ln1_b[i].reshape(1, d), alpha)
        p1, jj, e2, r2 = _route(h1b, peer_wq[i].astype(_BF), peer_subkeys[i].astype(_BF))
        ffn = _experts(h1b, peer_u[i].astype(_BF), peer_v[i].T.astype(_BF), p1, jj, e2, r2)
        h = _final(h1, h1b, ffn, p[i].reshape(t, -1).astype(_BF), ple_gate[i].astype(_BF),
                   ple_proj[i].astype(_BF), ln2_g[i].reshape(1, d), ln2_b[i].reshape(1, d), alpha)
    return h.reshape(bsz, seq, d)
```

```python
import functools

import numpy as np
import jax
import jax.numpy as jnp
from jax import lax
from jax.experimental import pallas as pl
from jax.experimental.pallas import tpu as pltpu

HEAD_DIM = 128
ROPE_THETA = 10000.0
BLOCK = 128
A_PATTERNS = ((128, 1), (512, 4), (2048, 16))
A_HEADS_PER_GROUP = 4
A_GROUPS = len(A_PATTERNS)
A_HEADS = A_HEADS_PER_GROUP * A_GROUPS
A_QKV = A_HEADS * HEAD_DIM
A_OUT = A_HEADS_PER_GROUP * HEAD_DIM
B_Q_HEADS = 8
B_KV_HEADS = 2
B_GROUP = B_Q_HEADS // B_KV_HEADS
B_Q = B_Q_HEADS * HEAD_DIM
B_KV = B_KV_HEADS * HEAD_DIM
PEER_HEADS = 8
PEER_HALF = 128
N_KEYS = 128
PEER_TOPK = 16
LN_EPS = 1e-5

V7X_VMEM_LIMIT = 56 * 1024 * 1024
A_TQ = max(d for _, d in A_PATTERNS) * BLOCK

_NT = (((1,), (1,)), ((), ()))
_BF = jnp.bfloat16
_F32 = jnp.float32


def _cparams(sem):
    return pltpu.CompilerParams(dimension_semantics=sem, vmem_limit_bytes=V7X_VMEM_LIMIT)


def _resident(block_shape, index_map):
    return pl.BlockSpec(block_shape, index_map, pipeline_mode=pl.Buffered(1))


def _inproj_kernel(x_ref, w_ref, cos_ref, sin_ref, o_ref, *, patterns):
    j = pl.program_id(1)
    acc = jnp.dot(x_ref[...], w_ref[...], preferred_element_type=_F32)
    tn = acc.shape[1]
    groups = {}
    for jb, pat in enumerate(patterns):
        groups.setdefault(pat, []).append(jb)
    for pat, blocks in groups.items():
        cond = functools.reduce(jnp.logical_or, [j == jb for jb in blocks])

        @pl.when(cond)
        def _(pat=pat):
            if not any(pat):
                o_ref[...] = acc
                return
            for c in range(tn // HEAD_DIM):
                t = acc[:, c * HEAD_DIM:(c + 1) * HEAD_DIM]
                if pat[c]:
                    t = t * cos_ref[...] + pltpu.roll(t, HEAD_DIM // 2, 1) * sin_ref[...]
                o_ref[:, c * HEAD_DIM:(c + 1) * HEAD_DIM] = t


def _inproj(xb, wb, cos2, sin2, seq, rope_cols, tm=1024, tn=512):
    m, k = xb.shape
    n = wb.shape[1]
    tm = min(tm, seq)
    patterns = tuple(
        tuple(any(lo <= (jb * tn + c * HEAD_DIM) < hi for lo, hi in rope_cols) for c in range(tn // HEAD_DIM))
        for jb in range(n // tn))
    sblocks = seq // tm
    return pl.pallas_call(
        functools.partial(_inproj_kernel, patterns=patterns),
        grid=(m // tm, n // tn),
        in_specs=[
            pl.BlockSpec((tm, k), lambda i, j: (i, 0)),
            pl.BlockSpec((k, tn), lambda i, j: (0, j)),
            pl.BlockSpec((tm, HEAD_DIM), lambda i, j: (i % sblocks, 0)),
            pl.BlockSpec((tm, HEAD_DIM), lambda i, j: (i % sblocks, 0)),
        ],
        out_specs=pl.BlockSpec((tm, tn), lambda i, j: (i, j)),
        out_shape=jax.ShapeDtypeStruct((m, n), _F32),
        compiler_params=_cparams(("parallel", "arbitrary")),
        name="inproj_rope",
    )(xb, wb, cos2, sin2)


def _band_scores(q, kc, kp, has_prev, strict_prev):
    scale = HEAD_DIM ** -0.5
    sc = lax.dot_general(q, kc, _NT, preferred_element_type=_F32) * scale
    sp = lax.dot_general(q, kp, _NT, preferred_element_type=_F32) * scale
    row = lax.broadcasted_iota(jnp.int32, sc.shape, 0) % BLOCK
    col = lax.broadcasted_iota(jnp.int32, sc.shape, 1)
    prev_ok = (col > row) if strict_prev else (col >= row)
    sc = jnp.where(col <= row, sc, -jnp.inf)
    sp = jnp.where(jnp.logical_and(prev_ok, has_prev), sp, -jnp.inf)
    return sc, sp


def _attn_a_kernel(q0, q1, q2, k0, k1, k2, v0, v1, v2, o_ref, acc_s, m_s, l_s):
    tq = o_ref.shape[0]
    t0 = pl.program_id(2) * tq
    refs = ((q0, k0, v0), (q1, k1, v1), (q2, k2, v2))
    for g, (window, dil) in enumerate(A_PATTERNS):
        q_ref, k_ref, v_ref = refs[g]
        span = BLOCK * dil

        def unit(u, carry, q_ref=q_ref, k_ref=k_ref, v_ref=v_ref, dil=dil, span=span, g=g):
            sb = u // dil
            r = u % dil
            loc = sb * span + r
            has_prev = (t0 + sb * span) > 0
            pstart = jnp.maximum(t0 + loc - span, r)
            q = q_ref[pl.ds(loc, BLOCK, stride=dil), :].astype(_BF)
            kc = k_ref[pl.ds(t0 + loc, BLOCK, stride=dil), :].astype(_BF)
            vc = v_ref[pl.ds(t0 + loc, BLOCK, stride=dil), :].astype(_BF)
            kp = k_ref[pl.ds(pstart, BLOCK, stride=dil), :].astype(_BF)
            vp = v_ref[pl.ds(pstart, BLOCK, stride=dil), :].astype(_BF)
            sc, sp = _band_scores(q, kc, kp, has_prev, strict_prev=False)
            m = jnp.maximum(jnp.max(sc, axis=1, keepdims=True), jnp.max(sp, axis=1, keepdims=True))
            pc = jnp.exp(sc - m)
            pp = jnp.exp(sp - m)
            l = jnp.sum(pc, axis=1, keepdims=True) + jnp.sum(pp, axis=1, keepdims=True)
            acc = (jnp.dot(pc.astype(_BF), vc, preferred_element_type=_F32)
                   + jnp.dot(pp.astype(_BF), vp, preferred_element_type=_F32))
            rows = pl.ds(loc, BLOCK, stride=dil)
            acc_s[g, rows, :] = acc
            m_s[g, rows, :] = jnp.broadcast_to(m, acc.shape)
            l_s[g, rows, :] = jnp.broadcast_to(l, acc.shape)
            return carry

        lax.fori_loop(0, tq // BLOCK, unit, 0)

    chunk = 256

    def combine(c, carry):
        rows = pl.ds(pl.multiple_of(c * chunk, chunk), chunk)
        ms = [m_s[g, rows, :] for g in range(A_GROUPS)]
        mx = functools.reduce(jnp.maximum, ms)
        ws = [jnp.exp(m - mx) for m in ms]
        num = sum(w * acc_s[g, rows, :] for g, w in enumerate(ws))
        den = sum(w * l_s[g, rows, :] for g, w in enumerate(ws))
        o_ref[rows, :] = (num / den).astype(o_ref.dtype)
        return carry

    lax.fori_loop(0, tq // chunk, combine, 0)


def _attn_a(z, bsz, seq):
    tq = A_TQ
    nh = A_HEADS_PER_GROUP
    kcol = A_QKV // HEAD_DIM
    vcol = 2 * A_QKV // HEAD_DIM

    def qspec(g):
        return pl.BlockSpec((None, tq, HEAD_DIM), lambda b, j, t, g=g: (b, t, g * nh + j))

    def kvspec(g, base):
        return pl.BlockSpec((None, seq, HEAD_DIM), lambda b, j, t, g=g, base=base: (b, 0, base + g * nh + j))

    in_specs = ([qspec(g) for g in range(A_GROUPS)] + [kvspec(g, kcol) for g in range(A_GROUPS)]
                + [kvspec(g, vcol) for g in range(A_GROUPS)])
    return pl.pallas_call(
        _attn_a_kernel,
        grid=(bsz, nh, seq // tq),
        in_specs=in_specs,
        out_specs=pl.BlockSpec((None, tq, HEAD_DIM), lambda b, j, t: (b, t, j)),
        out_shape=jax.ShapeDtypeStruct((bsz, seq, A_OUT), _BF),
        scratch_shapes=[pltpu.VMEM((A_GROUPS, tq, HEAD_DIM), _F32)] * 3,
        compiler_params=_cparams(("parallel", "parallel", "arbitrary")),
        name="attn_dilated",
    )(*([z] * 9))


def _attn_b_kernel(sink_ref, q_ref, k_ref, v_ref, o_ref):
    tq = o_ref.shape[0]
    t0 = pl.program_id(2) * tq
    c = pl.program_id(1)
    sink = jnp.concatenate(
        [jnp.full((BLOCK, 1), sink_ref[c * B_GROUP + g], _F32) for g in range(B_GROUP)], axis=0)

    def unit(u, carry):
        loc = pl.multiple_of(u * BLOCK, BLOCK)
        has_prev = (t0 + loc) > 0
        pstart = pl.multiple_of(jnp.maximum(t0 + loc - BLOCK, 0), BLOCK)
        q4 = q_ref[pl.ds(loc, BLOCK), :].astype(_BF)
        q = jnp.concatenate([q4[:, g * HEAD_DIM:(g + 1) * HEAD_DIM] for g in range(B_GROUP)], axis=0)
        kc = k_ref[pl.ds(pl.multiple_of(t0 + loc, BLOCK), BLOCK), :].astype(_BF)
        vc = v_ref[pl.ds(pl.multiple_of(t0 + loc, BLOCK), BLOCK), :].astype(_BF)
        kp = k_ref[pl.ds(pstart, BLOCK), :].astype(_BF)
        vp = v_ref[pl.ds(pstart, BLOCK), :].astype(_BF)
        sc, sp = _band_scores(q, kc, kp, has_prev, strict_prev=True)
        m = jnp.maximum(jnp.max(sc, axis=1, keepdims=True), jnp.max(sp, axis=1, keepdims=True))
        m = jnp.maximum(m, sink)
        pc = jnp.exp(sc - m)
        pp = jnp.exp(sp - m)
        denom = jnp.sum(pc, axis=1, keepdims=True) + jnp.sum(pp, axis=1, keepdims=True) + jnp.exp(sink - m)
        acc = (jnp.dot(pc.astype(_BF), vc, preferred_element_type=_F32)
               + jnp.dot(pp.astype(_BF), vp, preferred_element_type=_F32))
        o = (acc / denom).astype(o_ref.dtype)
        for g in range(B_GROUP):
            o_ref[pl.ds(loc, BLOCK), g * HEAD_DIM:(g + 1) * HEAD_DIM] = o[g * BLOCK:(g + 1) * BLOCK, :]
        return carry

    lax.fori_loop(0, tq // BLOCK, unit, 0)


def _attn_b(z, sinks, bsz, seq, tq=1024):
    tq = min(tq, seq)
    gq = B_GROUP * HEAD_DIM
    qcol = 3 * A_QKV // gq
    kcol = (3 * A_QKV + B_Q) // HEAD_DIM
    vcol = (3 * A_QKV + B_Q + B_KV) // HEAD_DIM
    return pl.pallas_call(
        _attn_b_kernel,
        grid=(bsz, B_KV_HEADS, seq // tq),
        in_specs=[
            pl.BlockSpec(memory_space=pltpu.SMEM),
            pl.BlockSpec((None, tq, gq), lambda b, c, t: (b, t, qcol + c)),
            pl.BlockSpec((None, seq, HEAD_DIM), lambda b, c, t: (b, 0, kcol + c)),
            pl.BlockSpec((None, seq, HEAD_DIM), lambda b, c, t: (b, 0, vcol + c)),
        ],
        out_specs=pl.BlockSpec((None, tq, gq), lambda b, c, t: (b, t, c)),
        out_shape=jax.ShapeDtypeStruct((bsz, seq, B_Q), _BF),
        compiler_params=_cparams(("parallel", "parallel", "arbitrary")),
        name="attn_swa_gqa",
    )(sinks, z, z, z)


def _layer_norm(h, g, b):
    mu = jnp.mean(h, axis=-1, keepdims=True)
    var = jnp.mean(jnp.square(h - mu), axis=-1, keepdims=True)
    return (h - mu) * lax.rsqrt(var + LN_EPS) * g + b


def _merge_kernel(oa_ref, ob_ref, ga_ref, gb_ref, x_ref, wa_ref, wb_ref, wo_ref, g_ref, b_ref,
                  h_ref, hb_ref, ht_ref, *, alpha):
    ma = jnp.dot(oa_ref[...], wa_ref[...], preferred_element_type=_F32)
    mb = jnp.dot(ob_ref[...], wb_ref[...], preferred_element_type=_F32)
    merged = jax.nn.sigmoid(ga_ref[...]) * ma + jax.nn.sigmoid(gb_ref[...]) * mb
    mix = jnp.dot(merged.astype(_BF), wo_ref[...], preferred_element_type=_F32)
    h = _layer_norm(alpha * x_ref[...] + mix, g_ref[...], b_ref[...])
    h_ref[...] = h
    hb_ref[...] = h.astype(_BF)
    ht_ref[...] = h.T.astype(_BF)


def _merge(oa, ob, z2, x2, wa, wb, wo, g, b, alpha, tm=256):
    t, d = x2.shape
    gcol = (3 * A_QKV + B_Q + 2 * B_KV) // d
    return pl.pallas_call(
        functools.partial(_merge_kernel, alpha=alpha),
        grid=(t // tm,),
        in_specs=[
            pl.BlockSpec((tm, A_OUT), lambda i: (i, 0)),
            pl.BlockSpec((tm, B_Q), lambda i: (i, 0)),
            pl.BlockSpec((tm, d), lambda i: (i, gcol)),
            pl.BlockSpec((tm, d), lambda i: (i, gcol + 1)),
            pl.BlockSpec((tm, d), lambda i: (i, 0)),
            _resident((A_OUT, d), lambda i: (0, 0)),
            _resident((B_Q, d), lambda i: (0, 0)),
            _resident((d, d), lambda i: (0, 0)),
            _resident((1, d), lambda i: (0, 0)),
            _resident((1, d), lambda i: (0, 0)),
        ],
        out_specs=[pl.BlockSpec((tm, d), lambda i: (i, 0)), pl.BlockSpec((tm, d), lambda i: (i, 0)),
                   pl.BlockSpec((d, tm), lambda i: (0, i))],
        out_shape=[jax.ShapeDtypeStruct((t, d), _F32), jax.ShapeDtypeStruct((t, d), _BF),
                   jax.ShapeDtypeStruct((d, t), _BF)],
        compiler_params=_cparams(("parallel",)),
        name="merge_outproj_ln",
    )(oa, ob, z2, z2, x2, wa, wb, wo, g, b)


def _extract_top(s, k):
    n = s.shape[0]
    iota = lax.broadcasted_iota(jnp.int32, s.shape, 0)
    rank = jnp.full(s.shape, float(k), _F32)
    vals, idxs = [], []
    for kk in range(k):
        m = jnp.max(s, axis=0, keepdims=True)
        idx = jnp.min(jnp.where(s == m, iota, n), axis=0, keepdims=True)
        sel = iota == idx
        rank = jnp.where(sel, float(kk), rank)
        s = jnp.where(sel, -jnp.inf, s)
        vals.append(m)
        idxs.append(idx)
    return jnp.concatenate(vals, axis=0), jnp.concatenate(idxs, axis=0), rank


def _route_kernel(h_ref, wq_ref, sk_ref, p1_ref, jj_ref, e2_ref, r2_ref, q_s):
    kk = PEER_TOPK
    q = jnp.dot(h_ref[...], wq_ref[...], preferred_element_type=_F32)
    for hc in range(2 * PEER_HEADS):
        q_s[hc] = q[:, hc * PEER_HALF:(hc + 1) * PEER_HALF].astype(_BF)

    def head(h, carry):
        s1 = lax.dot_general(sk_ref[h, 0], q_s[2 * h], _NT, preferred_element_type=_F32)
        s2 = lax.dot_general(sk_ref[h, 1], q_s[2 * h + 1], _NT, preferred_element_type=_F32)
        a1, _, r1 = _extract_top(s1, kk)
        a2, _, r2 = _extract_top(s2, kk)
        cand = jnp.concatenate([a1[i:i + 1, :] + a2 for i in range(kk)], axis=0)
        best, cidx, _ = _extract_top(cand, kk)
        ci = cidx // kk
        z = jnp.sum(jnp.exp(best - best[0:1, :]), axis=0, keepdims=True)
        jj = jnp.zeros(s1.shape, _F32)
        for i in range(kk):
            cnt = jnp.sum((ci == i).astype(_F32), axis=0, keepdims=True)
            jj = jnp.where(r1 == float(i), cnt, jj)
        p1_ref[h] = jnp.exp(s1 - a1[0:1, :]) / z
        jj_ref[h] = jj
        e2_ref[h] = jnp.exp(s2 - a2[0:1, :]).astype(_BF).reshape(e2_ref.shape[1:])
        r2_ref[h] = r2.astype(_BF).reshape(r2_ref.shape[1:])
        return carry

    lax.fori_loop(0, PEER_HEADS, head, 0)


def _route(hb, wq, sk, tm=256):
    t, d = hb.shape
    out = jax.ShapeDtypeStruct((PEER_HEADS, N_KEYS, t), _F32)
    ospec = pl.BlockSpec((PEER_HEADS, N_KEYS, tm), lambda i: (0, 0, i))
    outb = jax.ShapeDtypeStruct((PEER_HEADS, N_KEYS // 16, 16, t), _BF)
    obspec = pl.BlockSpec((PEER_HEADS, N_KEYS // 16, 16, tm), lambda i: (0, 0, 0, i))
    return pl.pallas_call(
        _route_kernel,
        grid=(t // tm,),
        in_specs=[
            pl.BlockSpec((tm, d), lambda i: (i, 0)),
            _resident(wq.shape, lambda i: (0, 0)),
            _resident(sk.shape, lambda i: (0, 0, 0, 0)),
        ],
        out_specs=[ospec, ospec, obspec, obspec],
        out_shape=[out, out, outb, outb],
        scratch_shapes=[pltpu.VMEM((2 * PEER_HEADS, tm, PEER_HALF), _BF)],
        compiler_params=_cparams(("parallel",)),
        name="peer_route",
    )(hb, wq, sk)


EXPERT_SUB = 256


def _gform(i1, p1_ref, jj_ref, e2_ref, r2_ref):
    tm = jj_ref.shape[2]
    gsum = None
    for h in range(PEER_HEADS):
        jrow = jnp.broadcast_to(jj_ref[h, pl.ds(i1, 1), :], (16, tm)).astype(_BF)[None]
        prow = jnp.broadcast_to(p1_ref[h, pl.ds(i1, 1), :], (16, tm)).astype(_BF)[None]
        term = jnp.where(r2_ref[h] < jrow, e2_ref[h], jnp.zeros((), _BF)) * prow
        gsum = term if gsum is None else gsum + term
    return gsum


def _experts_kernel(ht_ref, u_ref, vt_ref, p1_ref, jj_ref, e2_ref, r2_ref, o_ref, acc_s, act_s, w_s):
    e = pl.program_id(1)
    eb, tm = act_s.shape
    nsub = eb // N_KEYS

    @pl.when(e == 0)
    def _():
        def zero(r, c):
            acc_s[pl.ds(pl.multiple_of(r * 64, 64), 64), :] = jnp.zeros((64, tm), _F32)
            return c
        lax.fori_loop(0, acc_s.shape[0] // 64, zero, 0)

    for a in range(eb // EXPERT_SUB):
        rows = slice(a * EXPERT_SUB, (a + 1) * EXPERT_SUB)
        act_s[rows, :] = jnp.dot(u_ref[rows, :], ht_ref[...], preferred_element_type=_F32)
    for b in range(nsub):
        rows = slice(b * N_KEYS, (b + 1) * N_KEYS)
        x = act_s[rows, :]
        gelu = 0.5 * x * (1.0 + lax.erf(x * np.float32(np.sqrt(0.5))))
        g = _gform(e * nsub + b, p1_ref, jj_ref, e2_ref, r2_ref)
        w_s[rows, :] = (g * gelu.astype(_BF).reshape(N_KEYS // 16, 16, tm)).reshape(N_KEYS, tm)
    acc_s[...] += jnp.dot(vt_ref[...], w_s[...], preferred_element_type=_F32)

    @pl.when(e == pl.num_programs(1) - 1)
    def _():
        o_ref[...] = acc_s[...].T


def _experts(htb, ub, vtb, p1, jj, e2, r2, tm=512, eb=1024):
    d, t = htb.shape
    ne = ub.shape[0]
    rspec = pl.BlockSpec((PEER_HEADS, N_KEYS, tm), lambda i, e: (0, 0, i))
    bspec = pl.BlockSpec((PEER_HEADS, N_KEYS // 16, 16, tm), lambda i, e: (0, 0, 0, i))
    return pl.pallas_call(
        _experts_kernel,
        grid=(t // tm, ne // eb),
        in_specs=[
            pl.BlockSpec((d, tm), lambda i, e: (0, i)),
            pl.BlockSpec((eb, d), lambda i, e: (e, 0)),
            pl.BlockSpec((d, eb), lambda i, e: (0, e)),
            rspec, rspec, bspec, bspec,
        ],
        out_specs=pl.BlockSpec((tm, d), lambda i, e: (i, 0)),
        out_shape=jax.ShapeDtypeStruct((t, d), _F32),
        scratch_shapes=[pltpu.VMEM((d, tm), _F32), pltpu.VMEM((eb, tm), _F32), pltpu.VMEM((eb, tm), _BF)],
        compiler_params=_cparams(("parallel", "arbitrary")),
        name="peer_experts",
    )(htb, ub, vtb, p1, jj, e2, r2)


def _final_kernel(h_ref, hb_ref, f_ref, p_ref, wg_ref, wp_ref, g_ref, b_ref, o_ref, *, alpha):
    gate = jax.nn.sigmoid(jnp.dot(hb_ref[...], wg_ref[...], preferred_element_type=_F32))
    ple = gate * jnp.dot(p_ref[...], wp_ref[...], preferred_element_type=_F32)
    o_ref[...] = _layer_norm(alpha * h_ref[...] + f_ref[...] + ple, g_ref[...], b_ref[...])


def _final(h, hb, ffn, pb, wg, wp, g, b, alpha, tm=256):
    t, d = h.shape
    pd = pb.shape[1]
    return pl.pallas_call(
        functools.partial(_final_kernel, alpha=alpha),
        grid=(t // tm,),
        in_specs=[
            pl.BlockSpec((tm, d), lambda i: (i, 0)),
            pl.BlockSpec((tm, d), lambda i: (i, 0)),
            pl.BlockSpec((tm, d), lambda i: (i, 0)),
            pl.BlockSpec((tm, pd), lambda i: (i, 0)),
            _resident((d, d), lambda i: (0, 0)),
            _resident((pd, d), lambda i: (0, 0)),
            _resident((1, d), lambda i: (0, 0)),
            _resident((1, d), lambda i: (0, 0)),
        ],
        out_specs=pl.BlockSpec((tm, d), lambda i: (i, 0)),
        out_shape=jax.ShapeDtypeStruct((t, d), _F32),
        compiler_params=_cparams(("parallel",)),
        name="ple_final_ln",
    )(h, hb, ffn, pb, wg, wp, g, b)


def _rope_tables(seq):
    inv = 1.0 / (ROPE_THETA ** (jnp.arange(0, HEAD_DIM, 2, dtype=_F32) / HEAD_DIM))
    ang = jnp.arange(seq, dtype=_F32)[:, None] * inv[None, :]
    cos, sin = jnp.cos(ang), jnp.sin(ang)
    return jnp.concatenate([cos, cos], axis=1), jnp.concatenate([-sin, sin], axis=1)


def kernel(x, p, w_in, sinks, w_branch_a, w_branch_b, w_out, ln1_g, ln1_b, peer_wq, peer_subkeys, peer_u,
           peer_v, ple_gate, ple_proj, ln2_g, ln2_b):
    bsz, seq, d = x.shape
    depth = w_in.shape[0]
    t = bsz * seq
    alpha = (2 * depth) ** 0.25
    assert seq % A_TQ == 0 and w_in.shape[2] == 3 * A_QKV + B_Q + 2 * B_KV + 2 * d
    cos2, sin2 = _rope_tables(seq)
    rope_cols = ((0, 2 * A_QKV), (3 * A_QKV, 3 * A_QKV + B_Q + B_KV))
    h = x.reshape(t, d)
    for i in range(depth):
        z = _inproj(h.astype(_BF), w_in[i].astype(_BF), cos2, sin2, seq, rope_cols)
        z3 = z.reshape(bsz, seq, -1)
        oa = _attn_a(z3, bsz, seq).reshape(t, A_OUT)
        ob = _attn_b(z3, sinks[i].astype(_F32), bsz, seq).reshape(t, B_Q)
        h1, h1b, h1t = _merge(oa, ob, z, h, w_branch_a[i].astype(_BF), w_branch_b[i].astype(_BF),
                              w_out[i].astype(_BF), ln1_g[i].reshape(1, d), ln1_b[i].reshape(1, d), alpha)
        p1, jj, e2, r2 = _route(h1b, peer_wq[i].astype(_BF), peer_subkeys[i].astype(_BF))
        ffn = _experts(h1t, peer_u[i].astype(_BF), peer_v[i].T.astype(_BF), p1, jj, e2, r2)
        h = _final(h1, h1b, ffn, p[i].reshape(t, -1).astype(_BF), ple_gate[i].astype(_BF),
                   ple_proj[i].astype(_BF), ln2_g[i].reshape(1, d), ln2_b[i].reshape(1, d), alpha)
    return h.reshape(bsz, seq, d)
```

```python
import functools

import numpy as np
import jax
import jax.numpy as jnp
from jax import lax
from jax.experimental import pallas as pl
from jax.experimental.pallas import tpu as pltpu

HEAD_DIM = 128
ROPE_THETA = 10000.0
BLOCK = 128
A_PATTERNS = ((128, 1), (512, 4), (2048, 16))
A_HEADS_PER_GROUP = 4
A_GROUPS = len(A_PATTERNS)
A_HEADS = A_HEADS_PER_GROUP * A_GROUPS
A_QKV = A_HEADS * HEAD_DIM
A_OUT = A_HEADS_PER_GROUP * HEAD_DIM
B_Q_HEADS = 8
B_KV_HEADS = 2
B_GROUP = B_Q_HEADS // B_KV_HEADS
B_Q = B_Q_HEADS * HEAD_DIM
B_KV = B_KV_HEADS * HEAD_DIM
PEER_HEADS = 8
PEER_HALF = 128
N_KEYS = 128
PEER_TOPK = 16
LN_EPS = 1e-5

V7X_VMEM_LIMIT = 56 * 1024 * 1024
A_TQ = max(d for _, d in A_PATTERNS) * BLOCK

ATTN_UNROLL = 4

_NT = (((1,), (1,)), ((), ()))
_BF = jnp.bfloat16
_F32 = jnp.float32


def _cparams(sem):
    return pltpu.CompilerParams(dimension_semantics=sem, vmem_limit_bytes=V7X_VMEM_LIMIT)


def _resident(block_shape, index_map):
    return pl.BlockSpec(block_shape, index_map, pipeline_mode=pl.Buffered(1))


def _inproj_kernel(x_ref, w_ref, cos_ref, sin_ref, o_ref, *, patterns):
    j = pl.program_id(1)
    acc = jnp.dot(x_ref[...], w_ref[...], preferred_element_type=_F32)
    tn = acc.shape[1]
    groups = {}
    for jb, pat in enumerate(patterns):
        groups.setdefault(pat, []).append(jb)
    for pat, blocks in groups.items():
        cond = functools.reduce(jnp.logical_or, [j == jb for jb in blocks])

        @pl.when(cond)
        def _(pat=pat):
            if not any(pat):
                o_ref[...] = acc
                return
            for c in range(tn // HEAD_DIM):
                t = acc[:, c * HEAD_DIM:(c + 1) * HEAD_DIM]
                if pat[c]:
                    t = t * cos_ref[...] + pltpu.roll(t, HEAD_DIM // 2, 1) * sin_ref[...]
                o_ref[:, c * HEAD_DIM:(c + 1) * HEAD_DIM] = t


def _inproj(xb, wb, cos2, sin2, seq, rope_cols, tm=1024, tn=512):
    m, k = xb.shape
    n = wb.shape[1]
    tm = min(tm, seq)
    patterns = tuple(
        tuple(any(lo <= (jb * tn + c * HEAD_DIM) < hi for lo, hi in rope_cols) for c in range(tn // HEAD_DIM))
        for jb in range(n // tn))
    sblocks = seq // tm
    return pl.pallas_call(
        functools.partial(_inproj_kernel, patterns=patterns),
        grid=(m // tm, n // tn),
        in_specs=[
            pl.BlockSpec((tm, k), lambda i, j: (i, 0)),
            pl.BlockSpec((k, tn), lambda i, j: (0, j)),
            pl.BlockSpec((tm, HEAD_DIM), lambda i, j: (i % sblocks, 0)),
            pl.BlockSpec((tm, HEAD_DIM), lambda i, j: (i % sblocks, 0)),
        ],
        out_specs=pl.BlockSpec((tm, tn), lambda i, j: (i, j)),
        out_shape=jax.ShapeDtypeStruct((m, n), _F32),
        compiler_params=_cparams(("parallel", "arbitrary")),
        name="inproj_rope",
    )(xb, wb, cos2, sin2)


def _band_scores(q, kc, kp, has_prev, strict_prev):
    scale = HEAD_DIM ** -0.5
    sc = lax.dot_general(q, kc, _NT, preferred_element_type=_F32) * scale
    sp = lax.dot_general(q, kp, _NT, preferred_element_type=_F32) * scale
    row = lax.broadcasted_iota(jnp.int32, sc.shape, 0) % BLOCK
    col = lax.broadcasted_iota(jnp.int32, sc.shape, 1)
    prev_ok = (col > row) if strict_prev else (col >= row)
    sc = jnp.where(col <= row, sc, -jnp.inf)
    sp = jnp.where(jnp.logical_and(prev_ok, has_prev), sp, -jnp.inf)
    return sc, sp


def _attn_a_kernel(q0, q1, q2, k0, k1, k2, v0, v1, v2, o_ref, acc_s, m_s, l_s):
    tq = o_ref.shape[0]
    t0 = pl.program_id(2) * tq
    refs = ((q0, k0, v0), (q1, k1, v1), (q2, k2, v2))
    for g, (window, dil) in enumerate(A_PATTERNS):
        q_ref, k_ref, v_ref = refs[g]
        span = BLOCK * dil

        def unit(u, carry, q_ref=q_ref, k_ref=k_ref, v_ref=v_ref, dil=dil, span=span, g=g):
            sb = u // dil
            r = u % dil
            loc = sb * span + r
            has_prev = (t0 + sb * span) > 0
            pstart = jnp.maximum(t0 + loc - span, r)
            q = q_ref[pl.ds(loc, BLOCK, stride=dil), :].astype(_BF)
            kc = k_ref[pl.ds(t0 + loc, BLOCK, stride=dil), :].astype(_BF)
            vc = v_ref[pl.ds(t0 + loc, BLOCK, stride=dil), :].astype(_BF)
            kp = k_ref[pl.ds(pstart, BLOCK, stride=dil), :].astype(_BF)
            vp = v_ref[pl.ds(pstart, BLOCK, stride=dil), :].astype(_BF)
            sc, sp = _band_scores(q, kc, kp, has_prev, strict_prev=False)
            m = jnp.max(jnp.maximum(sc, sp), axis=1, keepdims=True)
            pc = jnp.exp(sc - m)
            pp = jnp.exp(sp - m)
            l = jnp.sum(pc + pp, axis=1, keepdims=True)
            acc = (jnp.dot(pc.astype(_BF), vc, preferred_element_type=_F32)
                   + jnp.dot(pp.astype(_BF), vp, preferred_element_type=_F32))
            rows = pl.ds(loc, BLOCK, stride=dil)
            acc_s[g, rows, :] = acc
            m_s[g, rows, :] = jnp.broadcast_to(m, acc.shape)
            l_s[g, rows, :] = jnp.broadcast_to(l, acc.shape)
            return carry

        lax.fori_loop(0, tq // BLOCK, unit, 0, unroll=ATTN_UNROLL)

    chunk = 256

    def combine(c, carry):
        rows = pl.ds(pl.multiple_of(c * chunk, chunk), chunk)
        ms = [m_s[g, rows, :] for g in range(A_GROUPS)]
        mx = functools.reduce(jnp.maximum, ms)
        ws = [jnp.exp(m - mx) for m in ms]
        num = sum(w * acc_s[g, rows, :] for g, w in enumerate(ws))
        den = sum(w * l_s[g, rows, :] for g, w in enumerate(ws))
        o_ref[rows, :] = (num / den).astype(o_ref.dtype)
        return carry

    lax.fori_loop(0, tq // chunk, combine, 0)


def _attn_a(z, bsz, seq):
    tq = A_TQ
    nh = A_HEADS_PER_GROUP
    kcol = A_QKV // HEAD_DIM
    vcol = 2 * A_QKV // HEAD_DIM

    def qspec(g):
        return pl.BlockSpec((None, tq, HEAD_DIM), lambda b, j, t, g=g: (b, t, g * nh + j))

    def kvspec(g, base):
        return pl.BlockSpec((None, seq, HEAD_DIM), lambda b, j, t, g=g, base=base: (b, 0, base + g * nh + j))

    in_specs = ([qspec(g) for g in range(A_GROUPS)] + [kvspec(g, kcol) for g in range(A_GROUPS)]
                + [kvspec(g, vcol) for g in range(A_GROUPS)])
    return pl.pallas_call(
        _attn_a_kernel,
        grid=(bsz, nh, seq // tq),
        in_specs=in_specs,
        out_specs=pl.BlockSpec((None, tq, HEAD_DIM), lambda b, j, t: (b, t, j)),
        out_shape=jax.ShapeDtypeStruct((bsz, seq, A_OUT), _BF),
        scratch_shapes=[pltpu.VMEM((A_GROUPS, tq, HEAD_DIM), _F32)] * 3,
        compiler_params=_cparams(("parallel", "parallel", "arbitrary")),
        name="attn_dilated",
    )(*([z] * 9))


def _attn_b_kernel(sink_ref, q_ref, k_ref, v_ref, o_ref):
    tq = o_ref.shape[0]
    t0 = pl.program_id(2) * tq
    c = pl.program_id(1)
    sink = jnp.concatenate(
        [jnp.full((BLOCK, 1), sink_ref[c * B_GROUP + g], _F32) for g in range(B_GROUP)], axis=0)

    def unit(u, carry):
        loc = pl.multiple_of(u * BLOCK, BLOCK)
        has_prev = (t0 + loc) > 0
        pstart = pl.multiple_of(jnp.maximum(t0 + loc - BLOCK, 0), BLOCK)
        q4 = q_ref[pl.ds(loc, BLOCK), :].astype(_BF)
        q = jnp.concatenate([q4[:, g * HEAD_DIM:(g + 1) * HEAD_DIM] for g in range(B_GROUP)], axis=0)
        kc = k_ref[pl.ds(pl.multiple_of(t0 + loc, BLOCK), BLOCK), :].astype(_BF)
        vc = v_ref[pl.ds(pl.multiple_of(t0 + loc, BLOCK), BLOCK), :].astype(_BF)
        kp = k_ref[pl.ds(pstart, BLOCK), :].astype(_BF)
        vp = v_ref[pl.ds(pstart, BLOCK), :].astype(_BF)
        sc, sp = _band_scores(q, kc, kp, has_prev, strict_prev=True)
        m = jnp.maximum(jnp.max(jnp.maximum(sc, sp), axis=1, keepdims=True), sink)
        pc = jnp.exp(sc - m)
        pp = jnp.exp(sp - m)
        denom = jnp.sum(pc + pp, axis=1, keepdims=True) + jnp.exp(sink - m)
        acc = (jnp.dot(pc.astype(_BF), vc, preferred_element_type=_F32)
               + jnp.dot(pp.astype(_BF), vp, preferred_element_type=_F32))
        o = (acc / denom).astype(o_ref.dtype)
        for g in range(B_GROUP):
            o_ref[pl.ds(loc, BLOCK), g * HEAD_DIM:(g + 1) * HEAD_DIM] = o[g * BLOCK:(g + 1) * BLOCK, :]
        return carry

    lax.fori_loop(0, tq // BLOCK, unit, 0)


def _attn_b(z, sinks, bsz, seq, tq=1024):
    tq = min(tq, seq)
    gq = B_GROUP * HEAD_DIM
    qcol = 3 * A_QKV // gq
    kcol = (3 * A_QKV + B_Q) // HEAD_DIM
    vcol = (3 * A_QKV + B_Q + B_KV) // HEAD_DIM
    return pl.pallas_call(
        _attn_b_kernel,
        grid=(bsz, B_KV_HEADS, seq // tq),
        in_specs=[
            pl.BlockSpec(memory_space=pltpu.SMEM),
            pl.BlockSpec((None, tq, gq), lambda b, c, t: (b, t, qcol + c)),
            pl.BlockSpec((None, seq, HEAD_DIM), lambda b, c, t: (b, 0, kcol + c)),
            pl.BlockSpec((None, seq, HEAD_DIM), lambda b, c, t: (b, 0, vcol + c)),
        ],
        out_specs=pl.BlockSpec((None, tq, gq), lambda b, c, t: (b, t, c)),
        out_shape=jax.ShapeDtypeStruct((bsz, seq, B_Q), _BF),
        compiler_params=_cparams(("parallel", "parallel", "arbitrary")),
        name="attn_swa_gqa",
    )(sinks, z, z, z)


def _layer_norm(h, g, b):
    mu = jnp.mean(h, axis=-1, keepdims=True)
    var = jnp.mean(jnp.square(h - mu), axis=-1, keepdims=True)
    return (h - mu) * lax.rsqrt(var + LN_EPS) * g + b


def _merge_kernel(oa_ref, ob_ref, ga_ref, gb_ref, x_ref, wa_ref, wb_ref, wo_ref, g_ref, b_ref,
                  h_ref, hb_ref, ht_ref, *, alpha):
    ma = jnp.dot(oa_ref[...], wa_ref[...], preferred_element_type=_F32)
    mb = jnp.dot(ob_ref[...], wb_ref[...], preferred_element_type=_F32)
    merged = jax.nn.sigmoid(ga_ref[...]) * ma + jax.nn.sigmoid(gb_ref[...]) * mb
    mix = jnp.dot(merged.astype(_BF), wo_ref[...], preferred_element_type=_F32)
    h = _layer_norm(alpha * x_ref[...] + mix, g_ref[...], b_ref[...])
    h_ref[...] = h
    hb_ref[...] = h.astype(_BF)
    ht_ref[...] = h.T.astype(_BF)


def _merge(oa, ob, z2, x2, wa, wb, wo, g, b, alpha, tm=256):
    t, d = x2.shape
    gcol = (3 * A_QKV + B_Q + 2 * B_KV) // d
    return pl.pallas_call(
        functools.partial(_merge_kernel, alpha=alpha),
        grid=(t // tm,),
        in_specs=[
            pl.BlockSpec((tm, A_OUT), lambda i: (i, 0)),
            pl.BlockSpec((tm, B_Q), lambda i: (i, 0)),
            pl.BlockSpec((tm, d), lambda i: (i, gcol)),
            pl.BlockSpec((tm, d), lambda i: (i, gcol + 1)),
            pl.BlockSpec((tm, d), lambda i: (i, 0)),
            _resident((A_OUT, d), lambda i: (0, 0)),
            _resident((B_Q, d), lambda i: (0, 0)),
            _resident((d, d), lambda i: (0, 0)),
            _resident((1, d), lambda i: (0, 0)),
            _resident((1, d), lambda i: (0, 0)),
        ],
        out_specs=[pl.BlockSpec((tm, d), lambda i: (i, 0)), pl.BlockSpec((tm, d), lambda i: (i, 0)),
                   pl.BlockSpec((d, tm), lambda i: (0, i))],
        out_shape=[jax.ShapeDtypeStruct((t, d), _F32), jax.ShapeDtypeStruct((t, d), _BF),
                   jax.ShapeDtypeStruct((d, t), _BF)],
        compiler_params=_cparams(("parallel",)),
        name="merge_outproj_ln",
    )(oa, ob, z2, z2, x2, wa, wb, wo, g, b)


ROUTE_LANES = 128


def _extract_top(s, k):
    n = s.shape[0]
    iota = lax.broadcasted_iota(jnp.int32, s.shape, 0)
    rank = jnp.full(s.shape, float(k), _F32)
    vals, idxs = [], []
    for kk in range(k):
        m = jnp.max(s, axis=0, keepdims=True)
        idx = jnp.min(jnp.where(s == m, iota, n), axis=0, keepdims=True)
        sel = iota == idx
        rank = jnp.where(sel, float(kk), rank)
        s = jnp.where(sel, -jnp.inf, s)
        vals.append(m)
        idxs.append(idx)
    return jnp.concatenate(vals, axis=0), jnp.concatenate(idxs, axis=0), rank


def _route_exact(s1, s2):
    kk = PEER_TOPK
    a1, _, r1 = _extract_top(s1, kk)
    a2, _, r2 = _extract_top(s2, kk)
    cand = jnp.concatenate([a1[i:i + 1, :] + a2 for i in range(kk)], axis=0)
    best, cidx, _ = _extract_top(cand, kk)
    ci = cidx // kk
    z = jnp.sum(jnp.exp(best - best[0:1, :]), axis=0, keepdims=True)
    jj = jnp.zeros(s1.shape, _F32)
    for i in range(kk):
        cnt = jnp.sum((ci == i).astype(_F32), axis=0, keepdims=True)
        jj = jnp.where(r1 == float(i), cnt, jj)
    return jnp.exp(s1 - a1[0:1, :]) / z, jj, jnp.exp(s2 - a2[0:1, :]), r2


def _strip_top(s, k, with_rank):
    rank = jnp.full(s.shape, float(k), _F32) if with_rank else None
    vals = []
    for kk in range(k):
        m = jnp.max(s, axis=0, keepdims=True)
        hit = s == m
        if with_rank:
            rank = jnp.where(hit, float(kk), rank)
        s = jnp.where(hit, -jnp.inf, s)
        vals.append(m)
    return jnp.concatenate(vals, axis=0), s, rank


_CAND_GROUPS = ((0, 16),) + tuple((i, 8) for i in range(1, 8))


def _route_fast(s1, s2):
    kk = PEER_TOPK
    a1, w1, _ = _strip_top(s1, kk, False)
    a2, w2, r2 = _strip_top(s2, kk, True)
    pieces = [a1[i:i + 1, :] + a2[0:w, :] for i, w in _CAND_GROUPS] + [a1[8:16, :] + a2[0:1, :]]
    cand = jnp.concatenate(pieces, axis=0)
    _, wc, _ = _strip_top(cand, kk, False)
    sel = wc == -jnp.inf
    self = sel.astype(_F32)
    z = jnp.sum(jnp.where(sel, jnp.exp(cand - cand[0:1, :]), 0.0), axis=0, keepdims=True)
    rows, off = [], 0
    for _, w in _CAND_GROUPS:
        rows.append(jnp.sum(self[off:off + w, :], axis=0, keepdims=True))
        off += w
    jcnt = jnp.concatenate(rows + [self[off:off + 8, :]], axis=0)
    jj = jnp.zeros(s1.shape, _F32)
    for i in range(kk):
        jj = jnp.where(s1 == a1[i:i + 1, :], jcnt[i:i + 1, :], jj)
    n1 = jnp.sum((w1 == -jnp.inf).astype(_F32), axis=0, keepdims=True)
    n2 = jnp.sum((w2 == -jnp.inf).astype(_F32), axis=0, keepdims=True)
    nc = jnp.sum(self, axis=0, keepdims=True)
    bad = jnp.maximum(jnp.maximum(jnp.abs(n1 - kk), jnp.abs(n2 - kk)), jnp.abs(nc - kk))
    return (jnp.exp(s1 - a1[0:1, :]) / z, jj, jnp.exp(s2 - a2[0:1, :]), r2), jnp.max(bad) > 0.0


def _route_kernel(h_ref, wq_ref, sk_ref, p1_ref, jj_ref, e2_ref, r2_ref, q_s):
    tm = h_ref.shape[0]
    q = jnp.dot(h_ref[...], wq_ref[...], preferred_element_type=_F32)
    for hc in range(2 * PEER_HEADS):
        q_s[hc] = q[:, hc * PEER_HALF:(hc + 1) * PEER_HALF].astype(_BF)

    def put(h, lanes, outs):
        p1, jj, e2, r2 = outs
        width = p1.shape[1]
        p1_ref[h, :, lanes] = p1
        jj_ref[h, :, lanes] = jj
        e2_ref[h, :, :, lanes] = e2.astype(_BF).reshape(N_KEYS // 16, 16, width)
        r2_ref[h, :, :, lanes] = r2.astype(_BF).reshape(N_KEYS // 16, 16, width)

    def head(h, carry):
        s1 = lax.dot_general(sk_ref[h, 0], q_s[2 * h], _NT, preferred_element_type=_F32)
        s2 = lax.dot_general(sk_ref[h, 1], q_s[2 * h + 1], _NT, preferred_element_type=_F32)
        tie = None
        for c in range(tm // ROUTE_LANES):
            lanes = slice(c * ROUTE_LANES, (c + 1) * ROUTE_LANES)
            outs, t = _route_fast(s1[:, lanes], s2[:, lanes])
            put(h, lanes, outs)
            tie = t if tie is None else jnp.logical_or(tie, t)

        @pl.when(tie)
        def _():
            put(h, slice(0, tm), _route_exact(s1, s2))
        return carry

    lax.fori_loop(0, PEER_HEADS, head, 0)


def _route(hb, wq, sk, tm=512):
    t, d = hb.shape
    out = jax.ShapeDtypeStruct((PEER_HEADS, N_KEYS, t), _F32)
    ospec = pl.BlockSpec((PEER_HEADS, N_KEYS, tm), lambda i: (0, 0, i))
    outb = jax.ShapeDtypeStruct((PEER_HEADS, N_KEYS // 16, 16, t), _BF)
    obspec = pl.BlockSpec((PEER_HEADS, N_KEYS // 16, 16, tm), lambda i: (0, 0, 0, i))
    return pl.pallas_call(
        _route_kernel,
        grid=(t // tm,),
        in_specs=[
            pl.BlockSpec((tm, d), lambda i: (i, 0)),
            _resident(wq.shape, lambda i: (0, 0)),
            _resident(sk.shape, lambda i: (0, 0, 0, 0)),
        ],
        out_specs=[ospec, ospec, obspec, obspec],
        out_shape=[out, out, outb, outb],
        scratch_shapes=[pltpu.VMEM((2 * PEER_HEADS, tm, PEER_HALF), _BF)],
        compiler_params=_cparams(("parallel",)),
        name="peer_route",
    )(hb, wq, sk)


EXPERT_SUB = 256


def _gform(i1, p1_ref, jj_ref, e2_ref, r2_ref):
    tm = jj_ref.shape[2]
    gsum = None
    for h in range(PEER_HEADS):
        jrow = jnp.broadcast_to(jj_ref[h, pl.ds(i1, 1), :], (16, tm)).astype(_BF)[None]
        prow = jnp.broadcast_to(p1_ref[h, pl.ds(i1, 1), :], (16, tm)).astype(_BF)[None]
        term = jnp.where(r2_ref[h] < jrow, e2_ref[h], jnp.zeros((), _BF)) * prow
        gsum = term if gsum is None else gsum + term
    return gsum


def _experts_kernel(ht_ref, u_ref, vt_ref, p1_ref, jj_ref, e2_ref, r2_ref, o_ref, acc_s, act_s, w_s):
    e = pl.program_id(1)
    eb, tm = act_s.shape
    nsub = eb // N_KEYS

    @pl.when(e == 0)
    def _():
        def zero(r, c):
            acc_s[pl.ds(pl.multiple_of(r * 64, 64), 64), :] = jnp.zeros((64, tm), _F32)
            return c
        lax.fori_loop(0, acc_s.shape[0] // 64, zero, 0)

    for a in range(eb // EXPERT_SUB):
        rows = slice(a * EXPERT_SUB, (a + 1) * EXPERT_SUB)
        act_s[rows, :] = jnp.dot(u_ref[rows, :], ht_ref[...], preferred_element_type=_F32)
    for b in range(nsub):
        rows = slice(b * N_KEYS, (b + 1) * N_KEYS)
        x = act_s[rows, :]
        gelu = 0.5 * x * (1.0 + lax.erf(x * np.float32(np.sqrt(0.5))))
        g = _gform(e * nsub + b, p1_ref, jj_ref, e2_ref, r2_ref)
        w_s[rows, :] = (g * gelu.astype(_BF).reshape(N_KEYS // 16, 16, tm)).reshape(N_KEYS, tm)
    acc_s[...] += jnp.dot(vt_ref[...], w_s[...], preferred_element_type=_F32)

    @pl.when(e == pl.num_programs(1) - 1)
    def _():
        o_ref[...] = acc_s[...].T


def _experts(htb, ub, vtb, p1, jj, e2, r2, tm=512, eb=1024):
    d, t = htb.shape
    ne = ub.shape[0]
    rspec = pl.BlockSpec((PEER_HEADS, N_KEYS, tm), lambda i, e: (0, 0, i))
    bspec = pl.BlockSpec((PEER_HEADS, N_KEYS // 16, 16, tm), lambda i, e: (0, 0, 0, i))
    return pl.pallas_call(
        _experts_kernel,
        grid=(t // tm, ne // eb),
        in_specs=[
            pl.BlockSpec((d, tm), lambda i, e: (0, i)),
            pl.BlockSpec((eb, d), lambda i, e: (e, 0)),
            pl.BlockSpec((d, eb), lambda i, e: (0, e)),
            rspec, rspec, bspec, bspec,
        ],
        out_specs=pl.BlockSpec((tm, d), lambda i, e: (i, 0)),
        out_shape=jax.ShapeDtypeStruct((t, d), _F32),
        scratch_shapes=[pltpu.VMEM((d, tm), _F32), pltpu.VMEM((eb, tm), _F32), pltpu.VMEM((eb, tm), _BF)],
        compiler_params=_cparams(("parallel", "arbitrary")),
        name="peer_experts",
    )(htb, ub, vtb, p1, jj, e2, r2)


def _final_kernel(h_ref, hb_ref, f_ref, p_ref, wg_ref, wp_ref, g_ref, b_ref, o_ref, *, alpha):
    gate = jax.nn.sigmoid(jnp.dot(hb_ref[...], wg_ref[...], preferred_element_type=_F32))
    ple = gate * jnp.dot(p_ref[...], wp_ref[...], preferred_element_type=_F32)
    o_ref[...] = _layer_norm(alpha * h_ref[...] + f_ref[...] + ple, g_ref[...], b_ref[...])


def _final(h, hb, ffn, pb, wg, wp, g, b, alpha, tm=256):
    t, d = h.shape
    pd = pb.shape[1]
    return pl.pallas_call(
        functools.partial(_final_kernel, alpha=alpha),
        grid=(t // tm,),
        in_specs=[
            pl.BlockSpec((tm, d), lambda i: (i, 0)),
            pl.BlockSpec((tm, d), lambda i: (i, 0)),
            pl.BlockSpec((tm, d), lambda i: (i, 0)),
            pl.BlockSpec((tm, pd), lambda i: (i, 0)),
            _resident((d, d), lambda i: (0, 0)),
            _resident((pd, d), lambda i: (0, 0)),
            _resident((1, d), lambda i: (0, 0)),
            _resident((1, d), lambda i: (0, 0)),
        ],
        out_specs=pl.BlockSpec((tm, d), lambda i: (i, 0)),
        out_shape=jax.ShapeDtypeStruct((t, d), _F32),
        compiler_params=_cparams(("parallel",)),
        name="ple_final_ln",
    )(h, hb, ffn, pb, wg, wp, g, b)


def _rope_tables(seq):
    inv = 1.0 / (ROPE_THETA ** (jnp.arange(0, HEAD_DIM, 2, dtype=_F32) / HEAD_DIM))
    ang = jnp.arange(seq, dtype=_F32)[:, None] * inv[None, :]
    cos, sin = jnp.cos(ang), jnp.sin(ang)
    return jnp.concatenate([cos, cos], axis=1), jnp.concatenate([-sin, sin], axis=1)


def kernel(x, p, w_in, sinks, w_branch_a, w_branch_b, w_out, ln1_g, ln1_b, peer_wq, peer_subkeys, peer_u,
           peer_v, ple_gate, ple_proj, ln2_g, ln2_b):
    bsz, seq, d = x.shape
    depth = w_in.shape[0]
    t = bsz * seq
    alpha = (2 * depth) ** 0.25
    assert seq % A_TQ == 0 and w_in.shape[2] == 3 * A_QKV + B_Q + 2 * B_KV + 2 * d
    cos2, sin2 = _rope_tables(seq)
    rope_cols = ((0, 2 * A_QKV), (3 * A_QKV, 3 * A_QKV + B_Q + B_KV))
    h = x.reshape(t, d)
    for i in range(depth):
        z = _inproj(h.astype(_BF), w_in[i].astype(_BF), cos2, sin2, seq, rope_cols)
        z3 = z.reshape(bsz, seq, -1)
        oa = _attn_a(z3, bsz, seq).reshape(t, A_OUT)
        ob = _attn_b(z3, sinks[i].astype(_F32), bsz, seq).reshape(t, B_Q)
        h1, h1b, h1t = _merge(oa, ob, z, h, w_branch_a[i].astype(_BF), w_branch_b[i].astype(_BF),
                              w_out[i].astype(_BF), ln1_g[i].reshape(1, d), ln1_b[i].reshape(1, d), alpha)
        p1, jj, e2, r2 = _route(h1b, peer_wq[i].astype(_BF), peer_subkeys[i].astype(_BF))
        ffn = _experts(h1t, peer_u[i].astype(_BF), peer_v[i].T.astype(_BF), p1, jj, e2, r2)
        h = _final(h1, h1b, ffn, p[i].reshape(t, -1).astype(_BF), ple_gate[i].astype(_BF),
                   ple_proj[i].astype(_BF), ln2_g[i].reshape(1, d), ln2_b[i].reshape(1, d), alpha)
    return h.reshape(bsz, seq, d)
```

```python
import functools

import numpy as np
import jax
import jax.numpy as jnp
from jax import lax
from jax.experimental import pallas as pl
from jax.experimental.pallas import tpu as pltpu

HEAD_DIM = 128
ROPE_THETA = 10000.0
BLOCK = 128
A_PATTERNS = ((128, 1), (512, 4), (2048, 16))
A_HEADS_PER_GROUP = 4
A_GROUPS = len(A_PATTERNS)
A_HEADS = A_HEADS_PER_GROUP * A_GROUPS
A_QKV = A_HEADS * HEAD_DIM
A_OUT = A_HEADS_PER_GROUP * HEAD_DIM
B_Q_HEADS = 8
B_KV_HEADS = 2
B_GROUP = B_Q_HEADS // B_KV_HEADS
B_Q = B_Q_HEADS * HEAD_DIM
B_KV = B_KV_HEADS * HEAD_DIM
PEER_HEADS = 8
PEER_HALF = 128
N_KEYS = 128
PEER_TOPK = 16
LN_EPS = 1e-5

V7X_VMEM_LIMIT = 56 * 1024 * 1024
A_TQ = max(d for _, d in A_PATTERNS) * BLOCK

ATTN_UNROLL = 8

_NT = (((1,), (1,)), ((), ()))
_BF = jnp.bfloat16
_F32 = jnp.float32


def _cparams(sem):
    return pltpu.CompilerParams(dimension_semantics=sem, vmem_limit_bytes=V7X_VMEM_LIMIT)


def _resident(block_shape, index_map):
    return pl.BlockSpec(block_shape, index_map, pipeline_mode=pl.Buffered(1))


INPROJ_ROWS = 128


def _inproj_kernel(x_ref, w_ref, cos_ref, sin_ref, o_ref, xb_s, *, patterns):
    j = pl.program_id(1)
    tm = x_ref.shape[0]

    @pl.when(j == 0)
    def _():
        def cast(r, c):
            rows = pl.ds(pl.multiple_of(r * BLOCK, BLOCK), BLOCK)
            xb_s[rows, :] = x_ref[rows, :].astype(_BF)
            return c
        lax.fori_loop(0, tm // BLOCK, cast, 0)

    def matmul(r):
        rows = slice(r * INPROJ_ROWS, (r + 1) * INPROJ_ROWS)
        return jnp.dot(xb_s[rows, :], w_ref[...], preferred_element_type=_F32)

    def epilogue(r, acc):
        rows = slice(r * INPROJ_ROWS, (r + 1) * INPROJ_ROWS)
        for c in range(acc.shape[1] // HEAD_DIM):
            t = acc[:, c * HEAD_DIM:(c + 1) * HEAD_DIM]
            blocks = [jb for jb, pat in enumerate(patterns) if pat[c]]
            if blocks:
                flag = functools.reduce(jnp.logical_or, [j == jb for jb in blocks])
                rope = t * cos_ref[rows, :] + pltpu.roll(t, HEAD_DIM // 2, 1) * sin_ref[rows, :]
                t = jnp.where(flag, rope, t)
            o_ref[rows, c * HEAD_DIM:(c + 1) * HEAD_DIM] = t

    nr = tm // INPROJ_ROWS
    acc = matmul(0)
    for r in range(nr):
        nxt = matmul(r + 1) if r + 1 < nr else None
        epilogue(r, acc)
        acc = nxt


def _inproj(x, wb, cos2, sin2, seq, rope_cols, tm=1024, tn=512):
    m, k = x.shape
    n = wb.shape[1]
    tm = min(tm, seq)
    patterns = tuple(
        tuple(any(lo <= (jb * tn + c * HEAD_DIM) < hi for lo, hi in rope_cols) for c in range(tn // HEAD_DIM))
        for jb in range(n // tn))
    sblocks = seq // tm
    return pl.pallas_call(
        functools.partial(_inproj_kernel, patterns=patterns),
        grid=(m // tm, n // tn),
        in_specs=[
            pl.BlockSpec((tm, k), lambda i, j: (i, 0)),
            pl.BlockSpec((k, tn), lambda i, j: (0, j)),
            pl.BlockSpec((tm, HEAD_DIM), lambda i, j: (i % sblocks, 0)),
            pl.BlockSpec((tm, HEAD_DIM), lambda i, j: (i % sblocks, 0)),
        ],
        out_specs=pl.BlockSpec((tm, tn), lambda i, j: (i, j)),
        out_shape=jax.ShapeDtypeStruct((m, n), _F32),
        scratch_shapes=[pltpu.VMEM((tm, k), _BF)],
        compiler_params=_cparams(("parallel", "arbitrary")),
        name="inproj_rope",
    )(x, wb, cos2, sin2)


def _band_scores(q, kc, kp, has_prev, strict_prev):
    scale = HEAD_DIM ** -0.5
    sc = lax.dot_general(q, kc, _NT, preferred_element_type=_F32) * scale
    sp = lax.dot_general(q, kp, _NT, preferred_element_type=_F32) * scale
    row = lax.broadcasted_iota(jnp.int32, sc.shape, 0) % BLOCK
    col = lax.broadcasted_iota(jnp.int32, sc.shape, 1)
    prev_ok = (col > row) if strict_prev else (col >= row)
    sc = jnp.where(col <= row, sc, -jnp.inf)
    sp = jnp.where(jnp.logical_and(prev_ok, has_prev), sp, -jnp.inf)
    return sc, sp


def _attn_a_kernel(q0, q1, q2, k0, k1, k2, v0, v1, v2, o_ref, acc_s, m_s, l_s):
    tq = o_ref.shape[0]
    t0 = pl.program_id(2) * tq
    refs = ((q0, k0, v0), (q1, k1, v1), (q2, k2, v2))
    for g, (window, dil) in enumerate(A_PATTERNS):
        q_ref, k_ref, v_ref = refs[g]
        span = BLOCK * dil

        def units(ug, carry, q_ref=q_ref, k_ref=k_ref, v_ref=v_ref, dil=dil, span=span, g=g):
            us = [ug * ATTN_UNROLL + i for i in range(ATTN_UNROLL)]
            locs = [(u // dil) * span + u % dil for u in us]
            scores = []
            for u, loc in zip(us, locs):
                has_prev = (t0 + (u // dil) * span) > 0
                pstart = jnp.maximum(t0 + loc - span, u % dil)
                q = q_ref[pl.ds(loc, BLOCK, stride=dil), :].astype(_BF)
                kc = k_ref[pl.ds(t0 + loc, BLOCK, stride=dil), :].astype(_BF)
                kp = k_ref[pl.ds(pstart, BLOCK, stride=dil), :].astype(_BF)
                scores.append(_band_scores(q, kc, kp, has_prev, strict_prev=False) + (pstart,))
            ms = [jnp.max(jnp.maximum(sc, sp), axis=1, keepdims=True) for sc, sp, _ in scores]
            probs = [(jnp.exp(sc - m), jnp.exp(sp - m)) for (sc, sp, _), m in zip(scores, ms)]
            ls = [jnp.sum(pc + pp, axis=1, keepdims=True) for pc, pp in probs]
            for loc, (_, _, pstart), m, (pc, pp), l in zip(locs, scores, ms, probs, ls):
                vc = v_ref[pl.ds(t0 + loc, BLOCK, stride=dil), :].astype(_BF)
                vp = v_ref[pl.ds(pstart, BLOCK, stride=dil), :].astype(_BF)
                acc = (jnp.dot(pc.astype(_BF), vc, preferred_element_type=_F32)
                       + jnp.dot(pp.astype(_BF), vp, preferred_element_type=_F32))
                rows = pl.ds(loc, BLOCK, stride=dil)
                acc_s[g, rows, :] = acc
                m_s[g, rows, :] = jnp.broadcast_to(m, acc.shape)
                l_s[g, rows, :] = jnp.broadcast_to(l, acc.shape)
            return carry

        lax.fori_loop(0, tq // (BLOCK * ATTN_UNROLL), units, 0)

    chunk = 256

    def combine(c, carry):
        rows = pl.ds(pl.multiple_of(c * chunk, chunk), chunk)
        ms = [m_s[g, rows, :] for g in range(A_GROUPS)]
        mx = functools.reduce(jnp.maximum, ms)
        ws = [jnp.exp(m - mx) for m in ms]
        num = sum(w * acc_s[g, rows, :] for g, w in enumerate(ws))
        den = sum(w * l_s[g, rows, :] for g, w in enumerate(ws))
        o_ref[rows, :] = (num / den).astype(o_ref.dtype)
        return carry

    lax.fori_loop(0, tq // chunk, combine, 0)


def _attn_a(z, bsz, seq):
    tq = A_TQ
    nh = A_HEADS_PER_GROUP
    kcol = A_QKV // HEAD_DIM
    vcol = 2 * A_QKV // HEAD_DIM

    def qspec(g):
        return pl.BlockSpec((None, tq, HEAD_DIM), lambda b, j, t, g=g: (b, t, g * nh + j))

    def kvspec(g, base):
        return pl.BlockSpec((None, seq, HEAD_DIM), lambda b, j, t, g=g, base=base: (b, 0, base + g * nh + j))

    in_specs = ([qspec(g) for g in range(A_GROUPS)] + [kvspec(g, kcol) for g in range(A_GROUPS)]
                + [kvspec(g, vcol) for g in range(A_GROUPS)])
    return pl.pallas_call(
        _attn_a_kernel,
        grid=(bsz, nh, seq // tq),
        in_specs=in_specs,
        out_specs=pl.BlockSpec((None, tq, HEAD_DIM), lambda b, j, t: (b, t, j)),
        out_shape=jax.ShapeDtypeStruct((bsz, seq, A_OUT), _BF),
        scratch_shapes=[pltpu.VMEM((A_GROUPS, tq, HEAD_DIM), _F32)] * 3,
        compiler_params=_cparams(("parallel", "parallel", "arbitrary")),
        name="attn_dilated",
    )(*([z] * 9))


def _attn_b_kernel(sink_ref, q_ref, k_ref, v_ref, o_ref):
    tq = o_ref.shape[0]
    t0 = pl.program_id(2) * tq
    c = pl.program_id(1)
    sink = jnp.concatenate(
        [jnp.full((BLOCK, 1), sink_ref[c * B_GROUP + g], _F32) for g in range(B_GROUP)], axis=0)

    def unit(u, carry):
        loc = pl.multiple_of(u * BLOCK, BLOCK)
        has_prev = (t0 + loc) > 0
        pstart = pl.multiple_of(jnp.maximum(t0 + loc - BLOCK, 0), BLOCK)
        q4 = q_ref[pl.ds(loc, BLOCK), :].astype(_BF)
        q = jnp.concatenate([q4[:, g * HEAD_DIM:(g + 1) * HEAD_DIM] for g in range(B_GROUP)], axis=0)
        kc = k_ref[pl.ds(pl.multiple_of(t0 + loc, BLOCK), BLOCK), :].astype(_BF)
        vc = v_ref[pl.ds(pl.multiple_of(t0 + loc, BLOCK), BLOCK), :].astype(_BF)
        kp = k_ref[pl.ds(pstart, BLOCK), :].astype(_BF)
        vp = v_ref[pl.ds(pstart, BLOCK), :].astype(_BF)
        sc, sp = _band_scores(q, kc, kp, has_prev, strict_prev=True)
        m = jnp.maximum(jnp.max(jnp.maximum(sc, sp), axis=1, keepdims=True), sink)
        pc = jnp.exp(sc - m)
        pp = jnp.exp(sp - m)
        denom = jnp.sum(pc + pp, axis=1, keepdims=True) + jnp.exp(sink - m)
        acc = (jnp.dot(pc.astype(_BF), vc, preferred_element_type=_F32)
               + jnp.dot(pp.astype(_BF), vp, preferred_element_type=_F32))
        o = (acc / denom).astype(o_ref.dtype)
        for g in range(B_GROUP):
            o_ref[pl.ds(loc, BLOCK), g * HEAD_DIM:(g + 1) * HEAD_DIM] = o[g * BLOCK:(g + 1) * BLOCK, :]
        return carry

    lax.fori_loop(0, tq // BLOCK, unit, 0)


def _attn_b(z, sinks, bsz, seq, tq=1024):
    tq = min(tq, seq)
    gq = B_GROUP * HEAD_DIM
    qcol = 3 * A_QKV // gq
    kcol = (3 * A_QKV + B_Q) // HEAD_DIM
    vcol = (3 * A_QKV + B_Q + B_KV) // HEAD_DIM
    return pl.pallas_call(
        _attn_b_kernel,
        grid=(bsz, B_KV_HEADS, seq // tq),
        in_specs=[
            pl.BlockSpec(memory_space=pltpu.SMEM),
            pl.BlockSpec((None, tq, gq), lambda b, c, t: (b, t, qcol + c)),
            pl.BlockSpec((None, seq, HEAD_DIM), lambda b, c, t: (b, 0, kcol + c)),
            pl.BlockSpec((None, seq, HEAD_DIM), lambda b, c, t: (b, 0, vcol + c)),
        ],
        out_specs=pl.BlockSpec((None, tq, gq), lambda b, c, t: (b, t, c)),
        out_shape=jax.ShapeDtypeStruct((bsz, seq, B_Q), _BF),
        compiler_params=_cparams(("parallel", "parallel", "arbitrary")),
        name="attn_swa_gqa",
    )(sinks, z, z, z)


def _layer_norm(h, g, b):
    mu = jnp.mean(h, axis=-1, keepdims=True)
    var = jnp.mean(jnp.square(h - mu), axis=-1, keepdims=True)
    return (h - mu) * lax.rsqrt(var + LN_EPS) * g + b


def _merge_kernel(oa_ref, ob_ref, ga_ref, gb_ref, x_ref, wa_ref, wb_ref, wo_ref, g_ref, b_ref,
                  h_ref, hb_ref, ht_ref, *, alpha):
    ma = jnp.dot(oa_ref[...], wa_ref[...], preferred_element_type=_F32)
    mb = jnp.dot(ob_ref[...], wb_ref[...], preferred_element_type=_F32)
    merged = jax.nn.sigmoid(ga_ref[...]) * ma + jax.nn.sigmoid(gb_ref[...]) * mb
    mix = jnp.dot(merged.astype(_BF), wo_ref[...], preferred_element_type=_F32)
    h = _layer_norm(alpha * x_ref[...] + mix, g_ref[...], b_ref[...])
    h_ref[...] = h
    hb_ref[...] = h.astype(_BF)
    ht_ref[...] = h.T.astype(_BF)


def _merge(oa, ob, z2, x2, wa, wb, wo, g, b, alpha, tm=256):
    t, d = x2.shape
    gcol = (3 * A_QKV + B_Q + 2 * B_KV) // d
    return pl.pallas_call(
        functools.partial(_merge_kernel, alpha=alpha),
        grid=(t // tm,),
        in_specs=[
            pl.BlockSpec((tm, A_OUT), lambda i: (i, 0)),
            pl.BlockSpec((tm, B_Q), lambda i: (i, 0)),
            pl.BlockSpec((tm, d), lambda i: (i, gcol)),
            pl.BlockSpec((tm, d), lambda i: (i, gcol + 1)),
            pl.BlockSpec((tm, d), lambda i: (i, 0)),
            _resident((A_OUT, d), lambda i: (0, 0)),
            _resident((B_Q, d), lambda i: (0, 0)),
            _resident((d, d), lambda i: (0, 0)),
            _resident((1, d), lambda i: (0, 0)),
            _resident((1, d), lambda i: (0, 0)),
        ],
        out_specs=[pl.BlockSpec((tm, d), lambda i: (i, 0)), pl.BlockSpec((tm, d), lambda i: (i, 0)),
                   pl.BlockSpec((d, tm), lambda i: (0, i))],
        out_shape=[jax.ShapeDtypeStruct((t, d), _F32), jax.ShapeDtypeStruct((t, d), _BF),
                   jax.ShapeDtypeStruct((d, t), _BF)],
        compiler_params=_cparams(("parallel",)),
        name="merge_outproj_ln",
    )(oa, ob, z2, z2, x2, wa, wb, wo, g, b)


ROUTE_LANES = 128


def _extract_top(s, k):
    n = s.shape[0]
    iota = lax.broadcasted_iota(jnp.int32, s.shape, 0)
    rank = jnp.full(s.shape, float(k), _F32)
    vals, idxs = [], []
    for kk in range(k):
        m = jnp.max(s, axis=0, keepdims=True)
        idx = jnp.min(jnp.where(s == m, iota, n), axis=0, keepdims=True)
        sel = iota == idx
        rank = jnp.where(sel, float(kk), rank)
        s = jnp.where(sel, -jnp.inf, s)
        vals.append(m)
        idxs.append(idx)
    return jnp.concatenate(vals, axis=0), jnp.concatenate(idxs, axis=0), rank


def _route_exact(s1, s2):
    kk = PEER_TOPK
    a1, _, r1 = _extract_top(s1, kk)
    a2, _, r2 = _extract_top(s2, kk)
    cand = jnp.concatenate([a1[i:i + 1, :] + a2 for i in range(kk)], axis=0)
    best, cidx, _ = _extract_top(cand, kk)
    ci = cidx // kk
    z = jnp.sum(jnp.exp(best - best[0:1, :]), axis=0, keepdims=True)
    jj = jnp.zeros(s1.shape, _F32)
    for i in range(kk):
        cnt = jnp.sum((ci == i).astype(_F32), axis=0, keepdims=True)
        jj = jnp.where(r1 == float(i), cnt, jj)
    return jnp.exp(s1 - a1[0:1, :]) / z, jj, jnp.exp(s2 - a2[0:1, :]), r2


def _strip_top(s, k, with_rank):
    rank = jnp.full(s.shape, float(k), _F32) if with_rank else None
    vals = []
    for kk in range(k):
        m = jnp.max(s, axis=0, keepdims=True)
        hit = s == m
        if with_rank:
            rank = jnp.where(hit, float(kk), rank)
        s = jnp.where(hit, -jnp.inf, s)
        vals.append(m)
    return jnp.concatenate(vals, axis=0), s, rank


_CAND_GROUPS = ((0, 16),) + tuple((i, 8) for i in range(1, 8))


def _route_fast(s1, s2):
    kk = PEER_TOPK
    a1, w1, _ = _strip_top(s1, kk, False)
    a2, w2, r2 = _strip_top(s2, kk, True)
    pieces = [a1[i:i + 1, :] + a2[0:w, :] for i, w in _CAND_GROUPS] + [a1[8:16, :] + a2[0:1, :]]
    cand = jnp.concatenate(pieces, axis=0)
    _, wc, _ = _strip_top(cand, kk, False)
    sel = wc == -jnp.inf
    self = sel.astype(_F32)
    z = jnp.sum(jnp.where(sel, jnp.exp(cand - cand[0:1, :]), 0.0), axis=0, keepdims=True)
    rows, off = [], 0
    for _, w in _CAND_GROUPS:
        rows.append(jnp.sum(self[off:off + w, :], axis=0, keepdims=True))
        off += w
    jcnt = jnp.concatenate(rows + [self[off:off + 8, :]], axis=0)
    jj = jnp.zeros(s1.shape, _F32)
    for i in range(kk):
        jj = jnp.where(s1 == a1[i:i + 1, :], jcnt[i:i + 1, :], jj)
    n1 = jnp.sum((w1 == -jnp.inf).astype(_F32), axis=0, keepdims=True)
    n2 = jnp.sum((w2 == -jnp.inf).astype(_F32), axis=0, keepdims=True)
    nc = jnp.sum(self, axis=0, keepdims=True)
    bad = jnp.maximum(jnp.maximum(jnp.abs(n1 - kk), jnp.abs(n2 - kk)), jnp.abs(nc - kk))
    return (jnp.exp(s1 - a1[0:1, :]) / z, jj, jnp.exp(s2 - a2[0:1, :]), r2), jnp.max(bad) > 0.0


def _route_kernel(h_ref, wq_ref, sk_ref, p1_ref, jj_ref, e2_ref, r2_ref, q_s):
    tm = h_ref.shape[0]
    q = jnp.dot(h_ref[...], wq_ref[...], preferred_element_type=_F32)
    for hc in range(2 * PEER_HEADS):
        q_s[hc] = q[:, hc * PEER_HALF:(hc + 1) * PEER_HALF].astype(_BF)

    def put(h, lanes, outs):
        p1, jj, e2, r2 = outs
        width = p1.shape[1]
        p1_ref[h, :, lanes] = p1
        jj_ref[h, :, lanes] = jj
        e2_ref[h, :, :, lanes] = e2.astype(_BF).reshape(N_KEYS // 16, 16, width)
        r2_ref[h, :, :, lanes] = r2.astype(_BF).reshape(N_KEYS // 16, 16, width)

    def head(h, carry):
        s1 = lax.dot_general(sk_ref[h, 0], q_s[2 * h], _NT, preferred_element_type=_F32)
        s2 = lax.dot_general(sk_ref[h, 1], q_s[2 * h + 1], _NT, preferred_element_type=_F32)
        tie = None
        for c in range(tm // ROUTE_LANES):
            lanes = slice(c * ROUTE_LANES, (c + 1) * ROUTE_LANES)
            outs, t = _route_fast(s1[:, lanes], s2[:, lanes])
            put(h, lanes, outs)
            tie = t if tie is None else jnp.logical_or(tie, t)

        @pl.when(tie)
        def _():
            put(h, slice(0, tm), _route_exact(s1, s2))
        return carry

    lax.fori_loop(0, PEER_HEADS, head, 0)


def _route(hb, wq, sk, tm=512):
    t, d = hb.shape
    out = jax.ShapeDtypeStruct((PEER_HEADS, N_KEYS, t), _F32)
    ospec = pl.BlockSpec((PEER_HEADS, N_KEYS, tm), lambda i: (0, 0, i))
    outb = jax.ShapeDtypeStruct((PEER_HEADS, N_KEYS // 16, 16, t), _BF)
    obspec = pl.BlockSpec((PEER_HEADS, N_KEYS // 16, 16, tm), lambda i: (0, 0, 0, i))
    return pl.pallas_call(
        _route_kernel,
        grid=(t // tm,),
        in_specs=[
            pl.BlockSpec((tm, d), lambda i: (i, 0)),
            _resident(wq.shape, lambda i: (0, 0)),
            _resident(sk.shape, lambda i: (0, 0, 0, 0)),
        ],
        out_specs=[ospec, ospec, obspec, obspec],
        out_shape=[out, out, outb, outb],
        scratch_shapes=[pltpu.VMEM((2 * PEER_HEADS, tm, PEER_HALF), _BF)],
        compiler_params=_cparams(("parallel",)),
        name="peer_route",
    )(hb, wq, sk)


EXPERT_SUB = 256


def _gform(i1, p1_ref, jj_ref, e2_ref, r2_ref):
    tm = jj_ref.shape[2]
    gsum = None
    for h in range(PEER_HEADS):
        jrow = jnp.broadcast_to(jj_ref[h, pl.ds(i1, 1), :], (16, tm)).astype(_BF)[None]
        prow = jnp.broadcast_to(p1_ref[h, pl.ds(i1, 1), :], (16, tm)).astype(_BF)[None]
        term = jnp.where(r2_ref[h] < jrow, e2_ref[h], jnp.zeros((), _BF)) * prow
        gsum = term if gsum is None else gsum + term
    return gsum


def _experts_kernel(ht_ref, u_ref, vt_ref, p1_ref, jj_ref, e2_ref, r2_ref, o_ref, acc_s, act_s, w_s):
    e = pl.program_id(1)
    eb, tm = act_s.shape
    nsub = eb // N_KEYS

    @pl.when(e == 0)
    def _():
        def zero(r, c):
            acc_s[pl.ds(pl.multiple_of(r * 64, 64), 64), :] = jnp.zeros((64, tm), _F32)
            return c
        lax.fori_loop(0, acc_s.shape[0] // 64, zero, 0)

    for a in range(eb // EXPERT_SUB):
        rows = slice(a * EXPERT_SUB, (a + 1) * EXPERT_SUB)
        act_s[rows, :] = jnp.dot(u_ref[rows, :], ht_ref[...], preferred_element_type=_F32)
    for b in range(nsub):
        rows = slice(b * N_KEYS, (b + 1) * N_KEYS)
        x = act_s[rows, :]
        gelu = 0.5 * x * (1.0 + lax.erf(x * np.float32(np.sqrt(0.5))))
        g = _gform(e * nsub + b, p1_ref, jj_ref, e2_ref, r2_ref)
        w_s[rows, :] = (g * gelu.astype(_BF).reshape(N_KEYS // 16, 16, tm)).reshape(N_KEYS, tm)
    acc_s[...] += jnp.dot(vt_ref[...], w_s[...], preferred_element_type=_F32)

    @pl.when(e == pl.num_programs(1) - 1)
    def _():
        o_ref[...] = acc_s[...].T


def _experts(htb, ub, vtb, p1, jj, e2, r2, tm=512, eb=1024):
    d, t = htb.shape
    ne = ub.shape[0]
    rspec = pl.BlockSpec((PEER_HEADS, N_KEYS, tm), lambda i, e: (0, 0, i))
    bspec = pl.BlockSpec((PEER_HEADS, N_KEYS // 16, 16, tm), lambda i, e: (0, 0, 0, i))
    return pl.pallas_call(
        _experts_kernel,
        grid=(t // tm, ne // eb),
        in_specs=[
            pl.BlockSpec((d, tm), lambda i, e: (0, i)),
            pl.BlockSpec((eb, d), lambda i, e: (e, 0)),
            pl.BlockSpec((d, eb), lambda i, e: (0, e)),
            rspec, rspec, bspec, bspec,
        ],
        out_specs=pl.BlockSpec((tm, d), lambda i, e: (i, 0)),
        out_shape=jax.ShapeDtypeStruct((t, d), _F32),
        scratch_shapes=[pltpu.VMEM((d, tm), _F32), pltpu.VMEM((eb, tm), _F32), pltpu.VMEM((eb, tm), _BF)],
        compiler_params=_cparams(("parallel", "arbitrary")),
        name="peer_experts",
    )(htb, ub, vtb, p1, jj, e2, r2)


def _final_kernel(h_ref, hb_ref, f_ref, p_ref, wg_ref, wp_ref, g_ref, b_ref, o_ref, *, alpha):
    gate = jax.nn.sigmoid(jnp.dot(hb_ref[...], wg_ref[...], preferred_element_type=_F32))
    ple = gate * jnp.dot(p_ref[...], wp_ref[...], preferred_element_type=_F32)
    o_ref[...] = _layer_norm(alpha * h_ref[...] + f_ref[...] + ple, g_ref[...], b_ref[...])


def _final(h, hb, ffn, pb, wg, wp, g, b, alpha, tm=256):
    t, d = h.shape
    pd = pb.shape[1]
    return pl.pallas_call(
        functools.partial(_final_kernel, alpha=alpha),
        grid=(t // tm,),
        in_specs=[
            pl.BlockSpec((tm, d), lambda i: (i, 0)),
            pl.BlockSpec((tm, d), lambda i: (i, 0)),
            pl.BlockSpec((tm, d), lambda i: (i, 0)),
            pl.BlockSpec((tm, pd), lambda i: (i, 0)),
            _resident((d, d), lambda i: (0, 0)),
            _resident((pd, d), lambda i: (0, 0)),
            _resident((1, d), lambda i: (0, 0)),
            _resident((1, d), lambda i: (0, 0)),
        ],
        out_specs=pl.BlockSpec((tm, d), lambda i: (i, 0)),
        out_shape=jax.ShapeDtypeStruct((t, d), _F32),
        compiler_params=_cparams(("parallel",)),
        name="ple_final_ln",
    )(h, hb, ffn, pb, wg, wp, g, b)


def _rope_tables(seq):
    inv = 1.0 / (ROPE_THETA ** (jnp.arange(0, HEAD_DIM, 2, dtype=_F32) / HEAD_DIM))
    ang = jnp.arange(seq, dtype=_F32)[:, None] * inv[None, :]
    cos, sin = jnp.cos(ang), jnp.sin(ang)
    return jnp.concatenate([cos, cos], axis=1), jnp.concatenate([-sin, sin], axis=1)


def kernel(x, p, w_in, sinks, w_branch_a, w_branch_b, w_out, ln1_g, ln1_b, peer_wq, peer_subkeys, peer_u,
           peer_v, ple_gate, ple_proj, ln2_g, ln2_b):
    bsz, seq, d = x.shape
    depth = w_in.shape[0]
    t = bsz * seq
    alpha = (2 * depth) ** 0.25
    assert seq % A_TQ == 0 and w_in.shape[2] == 3 * A_QKV + B_Q + 2 * B_KV + 2 * d
    cos2, sin2 = _rope_tables(seq)
    rope_cols = ((0, 2 * A_QKV), (3 * A_QKV, 3 * A_QKV + B_Q + B_KV))
    h = x.reshape(t, d)
    for i in range(depth):
        z = _inproj(h, w_in[i].astype(_BF), cos2, sin2, seq, rope_cols)
        z3 = z.reshape(bsz, seq, -1)
        oa = _attn_a(z3, bsz, seq).reshape(t, A_OUT)
        ob = _attn_b(z3, sinks[i].astype(_F32), bsz, seq).reshape(t, B_Q)
        h1, h1b, h1t = _merge(oa, ob, z, h, w_branch_a[i].astype(_BF), w_branch_b[i].astype(_BF),
                              w_out[i].astype(_BF), ln1_g[i].reshape(1, d), ln1_b[i].reshape(1, d), alpha)
        p1, jj, e2, r2 = _route(h1b, peer_wq[i].astype(_BF), peer_subkeys[i].astype(_BF))
        ffn = _experts(h1t, peer_u[i].astype(_BF), peer_v[i].T.astype(_BF), p1, jj, e2, r2)
        h = _final(h1, h1b, ffn, p[i].reshape(t, -1).astype(_BF), ple_gate[i].astype(_BF),
                   ple_proj[i].astype(_BF), ln2_g[i].reshape(1, d), ln2_b[i].reshape(1, d), alpha)
    return h.reshape(bsz, seq, d)
```

```python
import functools

import numpy as np
import jax
import jax.numpy as jnp
from jax import lax
from jax.experimental import pallas as pl
from jax.experimental.pallas import tpu as pltpu

HEAD_DIM = 128
ROPE_THETA = 10000.0
BLOCK = 128
A_PATTERNS = ((128, 1), (512, 4), (2048, 16))
A_HEADS_PER_GROUP = 4
A_GROUPS = len(A_PATTERNS)
A_HEADS = A_HEADS_PER_GROUP * A_GROUPS
A_QKV = A_HEADS * HEAD_DIM
A_OUT = A_HEADS_PER_GROUP * HEAD_DIM
B_Q_HEADS = 8
B_KV_HEADS = 2
B_GROUP = B_Q_HEADS // B_KV_HEADS
B_Q = B_Q_HEADS * HEAD_DIM
B_KV = B_KV_HEADS * HEAD_DIM
PEER_HEADS = 8
PEER_HALF = 128
N_KEYS = 128
PEER_TOPK = 16
LN_EPS = 1e-5

V7X_VMEM_LIMIT = 56 * 1024 * 1024
A_TQ = max(d for _, d in A_PATTERNS) * BLOCK

ATTN_UNROLL = 8

_NT = (((1,), (1,)), ((), ()))
_BF = jnp.bfloat16
_F32 = jnp.float32


def _cparams(sem):
    return pltpu.CompilerParams(dimension_semantics=sem, vmem_limit_bytes=V7X_VMEM_LIMIT)


def _resident(block_shape, index_map):
    return pl.BlockSpec(block_shape, index_map, pipeline_mode=pl.Buffered(1))


INPROJ_ROWS = 128


def _inproj_kernel(x_ref, w_ref, cos_ref, sin_ref, o_ref, xb_s, *, patterns):
    j = pl.program_id(1)
    tm = x_ref.shape[0]

    @pl.when(j == 0)
    def _():
        def cast(r, c):
            rows = pl.ds(pl.multiple_of(r * BLOCK, BLOCK), BLOCK)
            xb_s[rows, :] = x_ref[rows, :].astype(_BF)
            return c
        lax.fori_loop(0, tm // BLOCK, cast, 0)

    def matmul(r):
        rows = slice(r * INPROJ_ROWS, (r + 1) * INPROJ_ROWS)
        return jnp.dot(xb_s[rows, :], w_ref[...], preferred_element_type=_F32)

    def epilogue(r, acc):
        rows = slice(r * INPROJ_ROWS, (r + 1) * INPROJ_ROWS)
        for c in range(acc.shape[1] // HEAD_DIM):
            t = acc[:, c * HEAD_DIM:(c + 1) * HEAD_DIM]
            blocks = [jb for jb, pat in enumerate(patterns) if pat[c]]
            if blocks:
                flag = functools.reduce(jnp.logical_or, [j == jb for jb in blocks])
                rope = t * cos_ref[rows, :] + pltpu.roll(t, HEAD_DIM // 2, 1) * sin_ref[rows, :]
                t = jnp.where(flag, rope, t)
            o_ref[rows, c * HEAD_DIM:(c + 1) * HEAD_DIM] = t

    nr = tm // INPROJ_ROWS
    acc = matmul(0)
    for r in range(nr):
        nxt = matmul(r + 1) if r + 1 < nr else None
        epilogue(r, acc)
        acc = nxt


def _inproj(x, wb, cos2, sin2, seq, rope_cols, tm=1024, tn=1024):
    m, k = x.shape
    n = wb.shape[1]
    tm = min(tm, seq)
    tn = tn if n % tn == 0 else tn // 2
    assert n % tn == 0 and m % tm == 0
    patterns = tuple(
        tuple(any(lo <= (jb * tn + c * HEAD_DIM) < hi for lo, hi in rope_cols) for c in range(tn // HEAD_DIM))
        for jb in range(n // tn))
    sblocks = seq // tm
    return pl.pallas_call(
        functools.partial(_inproj_kernel, patterns=patterns),
        grid=(m // tm, n // tn),
        in_specs=[
            pl.BlockSpec((tm, k), lambda i, j: (i, 0)),
            pl.BlockSpec((k, tn), lambda i, j: (0, j)),
            pl.BlockSpec((tm, HEAD_DIM), lambda i, j: (i % sblocks, 0)),
            pl.BlockSpec((tm, HEAD_DIM), lambda i, j: (i % sblocks, 0)),
        ],
        out_specs=pl.BlockSpec((tm, tn), lambda i, j: (i, j)),
        out_shape=jax.ShapeDtypeStruct((m, n), _F32),
        scratch_shapes=[pltpu.VMEM((tm, k), _BF)],
        compiler_params=_cparams(("parallel", "arbitrary")),
        name="inproj_rope",
    )(x, wb, cos2, sin2)


def _band_scores(q, kc, kp, has_prev, strict_prev):
    scale = HEAD_DIM ** -0.5
    sc = lax.dot_general(q, kc, _NT, preferred_element_type=_F32) * scale
    sp = lax.dot_general(q, kp, _NT, preferred_element_type=_F32) * scale
    row = lax.broadcasted_iota(jnp.int32, sc.shape, 0) % BLOCK
    col = lax.broadcasted_iota(jnp.int32, sc.shape, 1)
    prev_ok = (col > row) if strict_prev else (col >= row)
    sc = jnp.where(col <= row, sc, -jnp.inf)
    sp = jnp.where(jnp.logical_and(prev_ok, has_prev), sp, -jnp.inf)
    return sc, sp


def _attn_a_kernel(q0, q1, q2, k0, k1, k2, v0, v1, v2, o_ref, acc_s, m_s, l_s):
    tq = o_ref.shape[0]
    t0 = pl.program_id(2) * tq
    refs = ((q0, k0, v0), (q1, k1, v1), (q2, k2, v2))
    for g, (window, dil) in enumerate(A_PATTERNS):
        q_ref, k_ref, v_ref = refs[g]
        span = BLOCK * dil

        def units(ug, carry, q_ref=q_ref, k_ref=k_ref, v_ref=v_ref, dil=dil, span=span, g=g):
            us = [ug * ATTN_UNROLL + i for i in range(ATTN_UNROLL)]
            locs = [(u // dil) * span + u % dil for u in us]
            scores = []
            for u, loc in zip(us, locs):
                has_prev = (t0 + (u // dil) * span) > 0
                pstart = jnp.maximum(t0 + loc - span, u % dil)
                q = q_ref[pl.ds(loc, BLOCK, stride=dil), :].astype(_BF)
                kc = k_ref[pl.ds(t0 + loc, BLOCK, stride=dil), :].astype(_BF)
                kp = k_ref[pl.ds(pstart, BLOCK, stride=dil), :].astype(_BF)
                scores.append(_band_scores(q, kc, kp, has_prev, strict_prev=False) + (pstart,))
            ms = [jnp.max(jnp.maximum(sc, sp), axis=1, keepdims=True) for sc, sp, _ in scores]
            probs = [(jnp.exp(sc - m), jnp.exp(sp - m)) for (sc, sp, _), m in zip(scores, ms)]
            ls = [jnp.sum(pc + pp, axis=1, keepdims=True) for pc, pp in probs]
            for loc, (_, _, pstart), m, (pc, pp), l in zip(locs, scores, ms, probs, ls):
                vc = v_ref[pl.ds(t0 + loc, BLOCK, stride=dil), :].astype(_BF)
                vp = v_ref[pl.ds(pstart, BLOCK, stride=dil), :].astype(_BF)
                acc = (jnp.dot(pc.astype(_BF), vc, preferred_element_type=_F32)
                       + jnp.dot(pp.astype(_BF), vp, preferred_element_type=_F32))
                rows = pl.ds(loc, BLOCK, stride=dil)
                acc_s[g, rows, :] = acc
                m_s[g, rows, :] = jnp.broadcast_to(m, acc.shape)
                l_s[g, rows, :] = jnp.broadcast_to(l, acc.shape)
            return carry

        lax.fori_loop(0, tq // (BLOCK * ATTN_UNROLL), units, 0)

    chunk = 256

    def combine(c, carry):
        rows = pl.ds(pl.multiple_of(c * chunk, chunk), chunk)
        ms = [m_s[g, rows, :] for g in range(A_GROUPS)]
        mx = functools.reduce(jnp.maximum, ms)
        ws = [jnp.exp(m - mx) for m in ms]
        num = sum(w * acc_s[g, rows, :] for g, w in enumerate(ws))
        den = sum(w * l_s[g, rows, :] for g, w in enumerate(ws))
        o_ref[rows, :] = (num / den).astype(o_ref.dtype)
        return carry

    lax.fori_loop(0, tq // chunk, combine, 0)


def _attn_a(z, bsz, seq):
    tq = A_TQ
    nh = A_HEADS_PER_GROUP
    kcol = A_QKV // HEAD_DIM
    vcol = 2 * A_QKV // HEAD_DIM

    def qspec(g):
        return pl.BlockSpec((None, tq, HEAD_DIM), lambda b, j, t, g=g: (b, t, g * nh + j))

    def kvspec(g, base):
        return pl.BlockSpec((None, seq, HEAD_DIM), lambda b, j, t, g=g, base=base: (b, 0, base + g * nh + j))

    in_specs = ([qspec(g) for g in range(A_GROUPS)] + [kvspec(g, kcol) for g in range(A_GROUPS)]
                + [kvspec(g, vcol) for g in range(A_GROUPS)])
    return pl.pallas_call(
        _attn_a_kernel,
        grid=(bsz, nh, seq // tq),
        in_specs=in_specs,
        out_specs=pl.BlockSpec((None, tq, HEAD_DIM), lambda b, j, t: (b, t, j)),
        out_shape=jax.ShapeDtypeStruct((bsz, seq, A_OUT), _BF),
        scratch_shapes=[pltpu.VMEM((A_GROUPS, tq, HEAD_DIM), _F32)] * 3,
        compiler_params=_cparams(("parallel", "parallel", "arbitrary")),
        name="attn_dilated",
    )(*([z] * 9))


def _attn_b_kernel(sink_ref, q_ref, k_ref, v_ref, o_ref):
    tq = o_ref.shape[0]
    t0 = pl.program_id(2) * tq
    c = pl.program_id(1)
    sink = jnp.concatenate(
        [jnp.full((BLOCK, 1), sink_ref[c * B_GROUP + g], _F32) for g in range(B_GROUP)], axis=0)

    def unit(u, carry):
        loc = pl.multiple_of(u * BLOCK, BLOCK)
        has_prev = (t0 + loc) > 0
        pstart = pl.multiple_of(jnp.maximum(t0 + loc - BLOCK, 0), BLOCK)
        q4 = q_ref[pl.ds(loc, BLOCK), :].astype(_BF)
        q = jnp.concatenate([q4[:, g * HEAD_DIM:(g + 1) * HEAD_DIM] for g in range(B_GROUP)], axis=0)
        kc = k_ref[pl.ds(pl.multiple_of(t0 + loc, BLOCK), BLOCK), :].astype(_BF)
        vc = v_ref[pl.ds(pl.multiple_of(t0 + loc, BLOCK), BLOCK), :].astype(_BF)
        kp = k_ref[pl.ds(pstart, BLOCK), :].astype(_BF)
        vp = v_ref[pl.ds(pstart, BLOCK), :].astype(_BF)
        sc, sp = _band_scores(q, kc, kp, has_prev, strict_prev=True)
        m = jnp.maximum(jnp.max(jnp.maximum(sc, sp), axis=1, keepdims=True), sink)
        pc = jnp.exp(sc - m)
        pp = jnp.exp(sp - m)
        denom = jnp.sum(pc + pp, axis=1, keepdims=True) + jnp.exp(sink - m)
        acc = (jnp.dot(pc.astype(_BF), vc, preferred_element_type=_F32)
               + jnp.dot(pp.astype(_BF), vp, preferred_element_type=_F32))
        o = (acc / denom).astype(o_ref.dtype)
        for g in range(B_GROUP):
            o_ref[pl.ds(loc, BLOCK), g * HEAD_DIM:(g + 1) * HEAD_DIM] = o[g * BLOCK:(g + 1) * BLOCK, :]
        return carry

    lax.fori_loop(0, tq // BLOCK, unit, 0)


def _attn_b(z, sinks, bsz, seq, tq=1024):
    tq = min(tq, seq)
    gq = B_GROUP * HEAD_DIM
    qcol = 3 * A_QKV // gq
    kcol = (3 * A_QKV + B_Q) // HEAD_DIM
    vcol = (3 * A_QKV + B_Q + B_KV) // HEAD_DIM
    return pl.pallas_call(
        _attn_b_kernel,
        grid=(bsz, B_KV_HEADS, seq // tq),
        in_specs=[
            pl.BlockSpec(memory_space=pltpu.SMEM),
            pl.BlockSpec((None, tq, gq), lambda b, c, t: (b, t, qcol + c)),
            pl.BlockSpec((None, seq, HEAD_DIM), lambda b, c, t: (b, 0, kcol + c)),
            pl.BlockSpec((None, seq, HEAD_DIM), lambda b, c, t: (b, 0, vcol + c)),
        ],
        out_specs=pl.BlockSpec((None, tq, gq), lambda b, c, t: (b, t, c)),
        out_shape=jax.ShapeDtypeStruct((bsz, seq, B_Q), _BF),
        compiler_params=_cparams(("parallel", "parallel", "arbitrary")),
        name="attn_swa_gqa",
    )(sinks, z, z, z)


ROW_GROUP = 128


def _layer_norm(h, g, b):
    mu = jnp.mean(h, axis=-1, keepdims=True)
    var = jnp.mean(jnp.square(h - mu), axis=-1, keepdims=True)
    return (h - mu) * lax.rsqrt(var + LN_EPS) * g + b


def _merge_kernel(oa_ref, ob_ref, ga_ref, gb_ref, x_ref, wa_ref, wb_ref, wo_ref, g_ref, b_ref,
                  h_ref, hb_ref, ht_ref, *, alpha):
    groups = [slice(r, r + ROW_GROUP) for r in range(0, x_ref.shape[0], ROW_GROUP)]

    def branches(rs):
        return (jnp.dot(oa_ref[rs, :], wa_ref[...], preferred_element_type=_F32),
                jnp.dot(ob_ref[rs, :], wb_ref[...], preferred_element_type=_F32))

    def mix(rs, ma, mb):
        merged = jax.nn.sigmoid(ga_ref[rs, :]) * ma + jax.nn.sigmoid(gb_ref[rs, :]) * mb
        return jnp.dot(merged.astype(_BF), wo_ref[...], preferred_element_type=_F32)

    def finish(rs, mixed):
        h = _layer_norm(alpha * x_ref[rs, :] + mixed, g_ref[...], b_ref[...])
        h_ref[rs, :] = h
        hb_ref[rs, :] = h.astype(_BF)
        ht_ref[:, rs] = h.T.astype(_BF)

    first = [branches(rs) for rs in groups]
    pending = None
    for rs, (ma, mb) in zip(groups, first):
        mixed = mix(rs, ma, mb)
        if pending is not None:
            finish(*pending)
        pending = (rs, mixed)
    finish(*pending)


def _merge(oa, ob, z2, x2, wa, wb, wo, g, b, alpha, tm=256):
    t, d = x2.shape
    gcol = (3 * A_QKV + B_Q + 2 * B_KV) // d
    return pl.pallas_call(
        functools.partial(_merge_kernel, alpha=alpha),
        grid=(t // tm,),
        in_specs=[
            pl.BlockSpec((tm, A_OUT), lambda i: (i, 0)),
            pl.BlockSpec((tm, B_Q), lambda i: (i, 0)),
            pl.BlockSpec((tm, d), lambda i: (i, gcol)),
            pl.BlockSpec((tm, d), lambda i: (i, gcol + 1)),
            pl.BlockSpec((tm, d), lambda i: (i, 0)),
            _resident((A_OUT, d), lambda i: (0, 0)),
            _resident((B_Q, d), lambda i: (0, 0)),
            _resident((d, d), lambda i: (0, 0)),
            _resident((1, d), lambda i: (0, 0)),
            _resident((1, d), lambda i: (0, 0)),
        ],
        out_specs=[pl.BlockSpec((tm, d), lambda i: (i, 0)), pl.BlockSpec((tm, d), lambda i: (i, 0)),
                   pl.BlockSpec((d, tm), lambda i: (0, i))],
        out_shape=[jax.ShapeDtypeStruct((t, d), _F32), jax.ShapeDtypeStruct((t, d), _BF),
                   jax.ShapeDtypeStruct((d, t), _BF)],
        compiler_params=_cparams(("parallel",)),
        name="merge_outproj_ln",
    )(oa, ob, z2, z2, x2, wa, wb, wo, g, b)


ROUTE_LANES = 256


def _extract_top(s, k):
    n = s.shape[0]
    iota = lax.broadcasted_iota(jnp.int32, s.shape, 0)
    rank = jnp.full(s.shape, float(k), _F32)
    vals, idxs = [], []
    for kk in range(k):
        m = jnp.max(s, axis=0, keepdims=True)
        idx = jnp.min(jnp.where(s == m, iota, n), axis=0, keepdims=True)
        sel = iota == idx
        rank = jnp.where(sel, float(kk), rank)
        s = jnp.where(sel, -jnp.inf, s)
        vals.append(m)
        idxs.append(idx)
    return jnp.concatenate(vals, axis=0), jnp.concatenate(idxs, axis=0), rank


def _route_exact(s1, s2):
    kk = PEER_TOPK
    a1, _, r1 = _extract_top(s1, kk)
    a2, _, r2 = _extract_top(s2, kk)
    cand = jnp.concatenate([a1[i:i + 1, :] + a2 for i in range(kk)], axis=0)
    best, cidx, _ = _extract_top(cand, kk)
    ci = cidx // kk
    z = jnp.sum(jnp.exp(best - best[0:1, :]), axis=0, keepdims=True)
    jj = jnp.zeros(s1.shape, _F32)
    for i in range(kk):
        cnt = jnp.sum((ci == i).astype(_F32), axis=0, keepdims=True)
        jj = jnp.where(r1 == float(i), cnt, jj)
    return jnp.exp(s1 - a1[0:1, :]) / z, jj, jnp.exp(s2 - a2[0:1, :]), r2


_STRIP_MARK = 2.0 ** 100


def _strip_top(s, k):
    vals = []
    for kk in range(k):
        m = jnp.max(s, axis=0, keepdims=True)
        s = jnp.where(s == m, -_STRIP_MARK * (1.0 + kk / k), s)
        vals.append(m)
    removed = s <= -_STRIP_MARK
    rank = jnp.where(removed, (s * (-1.0 / _STRIP_MARK) - 1.0) * k, float(k))
    return jnp.concatenate(vals, axis=0), removed, rank


_CAND_GROUPS = ((0, 16),) + tuple((i, 8) for i in range(1, 8))


def _route_fast(s1, s2):
    kk = PEER_TOPK
    a1, gone1, _ = _strip_top(s1, kk)
    a2, gone2, r2 = _strip_top(s2, kk)
    pieces = [a1[i:i + 1, :] + a2[0:w, :] for i, w in _CAND_GROUPS] + [a1[8:16, :] + a2[0:1, :]]
    cand = jnp.concatenate(pieces, axis=0)
    _, sel, _ = _strip_top(cand, kk)
    self = sel.astype(_F32)
    z = jnp.sum(jnp.where(sel, jnp.exp(cand - cand[0:1, :]), 0.0), axis=0, keepdims=True)
    rows, off = [], 0
    for _, w in _CAND_GROUPS:
        rows.append(jnp.sum(self[off:off + w, :], axis=0, keepdims=True))
        off += w
    jcnt = jnp.concatenate(rows + [self[off:off + 8, :]], axis=0)
    jj = jnp.zeros(s1.shape, _F32)
    for i in range(kk):
        jj = jnp.where(s1 == a1[i:i + 1, :], jcnt[i:i + 1, :], jj)
    n1 = jnp.sum(gone1.astype(_F32), axis=0, keepdims=True)
    n2 = jnp.sum(gone2.astype(_F32), axis=0, keepdims=True)
    nc = jnp.sum(self, axis=0, keepdims=True)
    bad = jnp.maximum(jnp.maximum(jnp.abs(n1 - kk), jnp.abs(n2 - kk)), jnp.abs(nc - kk))
    return (jnp.exp(s1 - a1[0:1, :]) / z, jj, jnp.exp(s2 - a2[0:1, :]), r2), jnp.max(bad) > 0.0


def _route_kernel(h_ref, wq_ref, sk_ref, p1_ref, jj_ref, e2_ref, r2_ref, q_s):
    tm = h_ref.shape[0]
    q = jnp.dot(h_ref[...], wq_ref[...], preferred_element_type=_F32)
    for hc in range(2 * PEER_HEADS):
        q_s[hc] = q[:, hc * PEER_HALF:(hc + 1) * PEER_HALF].astype(_BF)

    def put(h, lanes, outs):
        p1, jj, e2, r2 = outs
        width = p1.shape[1]
        p1_ref[h, :, lanes] = p1
        jj_ref[h, :, lanes] = jj
        e2_ref[h, :, :, lanes] = e2.astype(_BF).reshape(N_KEYS // 16, 16, width)
        r2_ref[h, :, :, lanes] = r2.astype(_BF).reshape(N_KEYS // 16, 16, width)

    def head(h, carry):
        s1 = lax.dot_general(sk_ref[h, 0], q_s[2 * h], _NT, preferred_element_type=_F32)
        s2 = lax.dot_general(sk_ref[h, 1], q_s[2 * h + 1], _NT, preferred_element_type=_F32)
        tie = None
        for c in range(tm // ROUTE_LANES):
            lanes = slice(c * ROUTE_LANES, (c + 1) * ROUTE_LANES)
            outs, t = _route_fast(s1[:, lanes], s2[:, lanes])
            put(h, lanes, outs)
            tie = t if tie is None else jnp.logical_or(tie, t)

        @pl.when(tie)
        def _():
            put(h, slice(0, tm), _route_exact(s1, s2))
        return carry

    lax.fori_loop(0, PEER_HEADS, head, 0)


def _route(hb, wq, sk, tm=512):
    t, d = hb.shape
    out = jax.ShapeDtypeStruct((PEER_HEADS, N_KEYS, t), _F32)
    ospec = pl.BlockSpec((PEER_HEADS, N_KEYS, tm), lambda i: (0, 0, i))
    outb = jax.ShapeDtypeStruct((PEER_HEADS, N_KEYS // 16, 16, t), _BF)
    obspec = pl.BlockSpec((PEER_HEADS, N_KEYS // 16, 16, tm), lambda i: (0, 0, 0, i))
    return pl.pallas_call(
        _route_kernel,
        grid=(t // tm,),
        in_specs=[
            pl.BlockSpec((tm, d), lambda i: (i, 0)),
            _resident(wq.shape, lambda i: (0, 0)),
            _resident(sk.shape, lambda i: (0, 0, 0, 0)),
        ],
        out_specs=[ospec, ospec, obspec, obspec],
        out_shape=[out, out, outb, outb],
        scratch_shapes=[pltpu.VMEM((2 * PEER_HEADS, tm, PEER_HALF), _BF)],
        compiler_params=_cparams(("parallel",)),
        name="peer_route",
    )(hb, wq, sk)


EXPERT_SUB = 256


def _gform(i1, p1_ref, jj_ref, e2_ref, r2_ref):
    tm = jj_ref.shape[2]
    gsum = None
    for h in range(PEER_HEADS):
        jrow = jnp.broadcast_to(jj_ref[h, pl.ds(i1, 1), :], (16, tm)).astype(_BF)[None]
        prow = jnp.broadcast_to(p1_ref[h, pl.ds(i1, 1), :], (16, tm)).astype(_BF)[None]
        term = jnp.where(r2_ref[h] < jrow, e2_ref[h], jnp.zeros((), _BF)) * prow
        gsum = term if gsum is None else gsum + term
    return gsum


def _experts_kernel(ht_ref, u_ref, vt_ref, p1_ref, jj_ref, e2_ref, r2_ref, o_ref, acc_s, act_s, w_s):
    e = pl.program_id(1)
    eb, tm = act_s.shape
    nsub = eb // N_KEYS

    @pl.when(e == 0)
    def _():
        def zero(r, c):
            acc_s[pl.ds(pl.multiple_of(r * 64, 64), 64), :] = jnp.zeros((64, tm), _F32)
            return c
        lax.fori_loop(0, acc_s.shape[0] // 64, zero, 0)

    for a in range(eb // EXPERT_SUB):
        rows = slice(a * EXPERT_SUB, (a + 1) * EXPERT_SUB)
        act_s[rows, :] = jnp.dot(u_ref[rows, :], ht_ref[...], preferred_element_type=_F32)
    for b in range(nsub):
        rows = slice(b * N_KEYS, (b + 1) * N_KEYS)
        x = act_s[rows, :]
        gelu = 0.5 * x * (1.0 + lax.erf(x * np.float32(np.sqrt(0.5))))
        g = _gform(e * nsub + b, p1_ref, jj_ref, e2_ref, r2_ref)
        w_s[rows, :] = (g * gelu.astype(_BF).reshape(N_KEYS // 16, 16, tm)).reshape(N_KEYS, tm)
    acc_s[...] += jnp.dot(vt_ref[...], w_s[...], preferred_element_type=_F32)

    @pl.when(e == pl.num_programs(1) - 1)
    def _():
        o_ref[...] = acc_s[...].T


def _experts(htb, ub, vtb, p1, jj, e2, r2, tm=512, eb=1024):
    d, t = htb.shape
    ne = ub.shape[0]
    rspec = pl.BlockSpec((PEER_HEADS, N_KEYS, tm), lambda i, e: (0, 0, i))
    bspec = pl.BlockSpec((PEER_HEADS, N_KEYS // 16, 16, tm), lambda i, e: (0, 0, 0, i))
    return pl.pallas_call(
        _experts_kernel,
        grid=(t // tm, ne // eb),
        in_specs=[
            pl.BlockSpec((d, tm), lambda i, e: (0, i)),
            pl.BlockSpec((eb, d), lambda i, e: (e, 0)),
            pl.BlockSpec((d, eb), lambda i, e: (0, e)),
            rspec, rspec, bspec, bspec,
        ],
        out_specs=pl.BlockSpec((tm, d), lambda i, e: (i, 0)),
        out_shape=jax.ShapeDtypeStruct((t, d), _F32),
        scratch_shapes=[pltpu.VMEM((d, tm), _F32), pltpu.VMEM((eb, tm), _F32), pltpu.VMEM((eb, tm), _BF)],
        compiler_params=_cparams(("parallel", "arbitrary")),
        name="peer_experts",
    )(htb, ub, vtb, p1, jj, e2, r2)


def _final_kernel(h_ref, hb_ref, f_ref, p_ref, wg_ref, wp_ref, g_ref, b_ref, o_ref, *, alpha):
    groups = [slice(r, r + ROW_GROUP) for r in range(0, h_ref.shape[0], ROW_GROUP)]
    dots = [(jnp.dot(hb_ref[rs, :], wg_ref[...], preferred_element_type=_F32),
             jnp.dot(p_ref[rs, :], wp_ref[...], preferred_element_type=_F32)) for rs in groups]
    for rs, (gate, proj) in zip(groups, dots):
        ple = jax.nn.sigmoid(gate) * proj
        o_ref[rs, :] = _layer_norm(alpha * h_ref[rs, :] + f_ref[rs, :] + ple, g_ref[...], b_ref[...])


def _final(h, hb, ffn, pb, wg, wp, g, b, alpha, tm=256):
    t, d = h.shape
    pd = pb.shape[1]
    return pl.pallas_call(
        functools.partial(_final_kernel, alpha=alpha),
        grid=(t // tm,),
        in_specs=[
            pl.BlockSpec((tm, d), lambda i: (i, 0)),
            pl.BlockSpec((tm, d), lambda i: (i, 0)),
            pl.BlockSpec((tm, d), lambda i: (i, 0)),
            pl.BlockSpec((tm, pd), lambda i: (i, 0)),
            _resident((d, d), lambda i: (0, 0)),
            _resident((pd, d), lambda i: (0, 0)),
            _resident((1, d), lambda i: (0, 0)),
            _resident((1, d), lambda i: (0, 0)),
        ],
        out_specs=pl.BlockSpec((tm, d), lambda i: (i, 0)),
        out_shape=jax.ShapeDtypeStruct((t, d), _F32),
        compiler_params=_cparams(("parallel",)),
        name="ple_final_ln",
    )(h, hb, ffn, pb, wg, wp, g, b)


def _rope_tables(seq):
    inv = 1.0 / (ROPE_THETA ** (jnp.arange(0, HEAD_DIM, 2, dtype=_F32) / HEAD_DIM))
    ang = jnp.arange(seq, dtype=_F32)[:, None] * inv[None, :]
    cos, sin = jnp.cos(ang), jnp.sin(ang)
    return jnp.concatenate([cos, cos], axis=1), jnp.concatenate([-sin, sin], axis=1)


def kernel(x, p, w_in, sinks, w_branch_a, w_branch_b, w_out, ln1_g, ln1_b, peer_wq, peer_subkeys, peer_u,
           peer_v, ple_gate, ple_proj, ln2_g, ln2_b):
    bsz, seq, d = x.shape
    depth = w_in.shape[0]
    t = bsz * seq
    alpha = (2 * depth) ** 0.25
    assert seq % A_TQ == 0 and w_in.shape[2] == 3 * A_QKV + B_Q + 2 * B_KV + 2 * d
    cos2, sin2 = _rope_tables(seq)
    rope_cols = ((0, 2 * A_QKV), (3 * A_QKV, 3 * A_QKV + B_Q + B_KV))
    h = x.reshape(t, d)
    for i in range(depth):
        z = _inproj(h, w_in[i].astype(_BF), cos2, sin2, seq, rope_cols)
        z3 = z.reshape(bsz, seq, -1)
        oa = _attn_a(z3, bsz, seq).reshape(t, A_OUT)
        ob = _attn_b(z3, sinks[i].astype(_F32), bsz, seq).reshape(t, B_Q)
        h1, h1b, h1t = _merge(oa, ob, z, h, w_branch_a[i].astype(_BF), w_branch_b[i].astype(_BF),
                              w_out[i].astype(_BF), ln1_g[i].reshape(1, d), ln1_b[i].reshape(1, d), alpha)
        p1, jj, e2, r2 = _route(h1b, peer_wq[i].astype(_BF), peer_subkeys[i].astype(_BF))
        ffn = _experts(h1t, peer_u[i].astype(_BF), peer_v[i].T.astype(_BF), p1, jj, e2, r2)
        h = _final(h1, h1b, ffn, p[i].reshape(t, -1).astype(_BF), ple_gate[i].astype(_BF),
                   ple_proj[i].astype(_BF), ln2_g[i].reshape(1, d), ln2_b[i].reshape(1, d), alpha)
    return h.reshape(bsz, seq, d)
```

```python
import functools

import numpy as np
import jax
import jax.numpy as jnp
from jax import lax
from jax.experimental import pallas as pl
from jax.experimental.pallas import tpu as pltpu

HEAD_DIM = 128
ROPE_THETA = 10000.0
BLOCK = 128
A_PATTERNS = ((128, 1), (512, 4), (2048, 16))
A_HEADS_PER_GROUP = 4
A_GROUPS = len(A_PATTERNS)
A_HEADS = A_HEADS_PER_GROUP * A_GROUPS
A_QKV = A_HEADS * HEAD_DIM
A_OUT = A_HEADS_PER_GROUP * HEAD_DIM
B_Q_HEADS = 8
B_KV_HEADS = 2
B_GROUP = B_Q_HEADS // B_KV_HEADS
B_Q = B_Q_HEADS * HEAD_DIM
B_KV = B_KV_HEADS * HEAD_DIM
PEER_HEADS = 8
PEER_HALF = 128
N_KEYS = 128
PEER_TOPK = 16
LN_EPS = 1e-5

V7X_VMEM_LIMIT = 60 * 1024 * 1024
A_TQ = max(d for _, d in A_PATTERNS) * BLOCK

ATTN_UNROLL = 8

_NT = (((1,), (1,)), ((), ()))
_BF = jnp.bfloat16
_F32 = jnp.float32


def _cparams(sem):
    return pltpu.CompilerParams(dimension_semantics=sem, vmem_limit_bytes=V7X_VMEM_LIMIT)


def _resident(block_shape, index_map):
    return pl.BlockSpec(block_shape, index_map, pipeline_mode=pl.Buffered(1))


INPROJ_ROWS = 128


def _inproj_kernel(x_ref, w_ref, cos_ref, sin_ref, o_ref, xb_s, *, patterns):
    j = pl.program_id(1)
    tm = x_ref.shape[0]

    @pl.when(j == 0)
    def _():
        def cast(r, c):
            rows = pl.ds(pl.multiple_of(r * BLOCK, BLOCK), BLOCK)
            xb_s[rows, :] = x_ref[rows, :].astype(_BF)
            return c
        lax.fori_loop(0, tm // BLOCK, cast, 0)

    def matmul(r):
        rows = slice(r * INPROJ_ROWS, (r + 1) * INPROJ_ROWS)
        return jnp.dot(xb_s[rows, :], w_ref[...], preferred_element_type=_F32)

    def epilogue(r, acc):
        rows = slice(r * INPROJ_ROWS, (r + 1) * INPROJ_ROWS)
        for c in range(acc.shape[1] // HEAD_DIM):
            t = acc[:, c * HEAD_DIM:(c + 1) * HEAD_DIM]
            blocks = [jb for jb, pat in enumerate(patterns) if pat[c]]
            if blocks:
                flag = functools.reduce(jnp.logical_or, [j == jb for jb in blocks])
                rope = t * cos_ref[rows, :] + pltpu.roll(t, HEAD_DIM // 2, 1) * sin_ref[rows, :]
                t = jnp.where(flag, rope, t)
            o_ref[rows, c * HEAD_DIM:(c + 1) * HEAD_DIM] = t

    nr = tm // INPROJ_ROWS
    acc = matmul(0)
    for r in range(nr):
        nxt = matmul(r + 1) if r + 1 < nr else None
        epilogue(r, acc)
        acc = nxt


def _inproj(x, wb, cos2, sin2, seq, rope_cols, tm=1024, tn=1024):
    m, k = x.shape
    n = wb.shape[1]
    tm = min(tm, seq)
    tn = tn if n % tn == 0 else tn // 2
    assert n % tn == 0 and m % tm == 0
    patterns = tuple(
        tuple(any(lo <= (jb * tn + c * HEAD_DIM) < hi for lo, hi in rope_cols) for c in range(tn // HEAD_DIM))
        for jb in range(n // tn))
    sblocks = seq // tm
    return pl.pallas_call(
        functools.partial(_inproj_kernel, patterns=patterns),
        grid=(m // tm, n // tn),
        in_specs=[
            pl.BlockSpec((tm, k), lambda i, j: (i, 0)),
            pl.BlockSpec((k, tn), lambda i, j: (0, j)),
            pl.BlockSpec((tm, HEAD_DIM), lambda i, j: (i % sblocks, 0)),
            pl.BlockSpec((tm, HEAD_DIM), lambda i, j: (i % sblocks, 0)),
        ],
        out_specs=pl.BlockSpec((tm, tn), lambda i, j: (i, j)),
        out_shape=jax.ShapeDtypeStruct((m, n), _F32),
        scratch_shapes=[pltpu.VMEM((tm, k), _BF)],
        compiler_params=_cparams(("parallel", "arbitrary")),
        name="inproj_rope",
    )(x, wb, cos2, sin2)


def _band_scores(q, kc, kp, has_prev, strict_prev):
    scale = HEAD_DIM ** -0.5
    sc = lax.dot_general(q, kc, _NT, preferred_element_type=_F32) * scale
    sp = lax.dot_general(q, kp, _NT, preferred_element_type=_F32) * scale
    row = lax.broadcasted_iota(jnp.int32, sc.shape, 0) % BLOCK
    col = lax.broadcasted_iota(jnp.int32, sc.shape, 1)
    prev_ok = (col > row) if strict_prev else (col >= row)
    sc = jnp.where(col <= row, sc, -jnp.inf)
    sp = jnp.where(jnp.logical_and(prev_ok, has_prev), sp, -jnp.inf)
    return sc, sp


def _attn_a_kernel(q0, q1, q2, k0, k1, k2, v0, v1, v2, o_ref, acc_s, m_s, l_s):
    tq = o_ref.shape[0]
    t0 = pl.program_id(2) * tq
    refs = ((q0, k0, v0), (q1, k1, v1), (q2, k2, v2))
    for g, (window, dil) in enumerate(A_PATTERNS):
        q_ref, k_ref, v_ref = refs[g]
        span = BLOCK * dil

        def units(ug, carry, q_ref=q_ref, k_ref=k_ref, v_ref=v_ref, dil=dil, span=span, g=g):
            us = [ug * ATTN_UNROLL + i for i in range(ATTN_UNROLL)]
            locs = [(u // dil) * span + u % dil for u in us]
            scores = []
            for u, loc in zip(us, locs):
                has_prev = (t0 + (u // dil) * span) > 0
                pstart = jnp.maximum(t0 + loc - span, u % dil)
                q = q_ref[pl.ds(loc, BLOCK, stride=dil), :].astype(_BF)
                kc = k_ref[pl.ds(t0 + loc, BLOCK, stride=dil), :].astype(_BF)
                kp = k_ref[pl.ds(pstart, BLOCK, stride=dil), :].astype(_BF)
                scores.append(_band_scores(q, kc, kp, has_prev, strict_prev=False) + (pstart,))
            ms = [jnp.max(jnp.maximum(sc, sp), axis=1, keepdims=True) for sc, sp, _ in scores]
            probs = [(jnp.exp(sc - m), jnp.exp(sp - m)) for (sc, sp, _), m in zip(scores, ms)]
            ls = [jnp.sum(pc + pp, axis=1, keepdims=True) for pc, pp in probs]
            for loc, (_, _, pstart), m, (pc, pp), l in zip(locs, scores, ms, probs, ls):
                vc = v_ref[pl.ds(t0 + loc, BLOCK, stride=dil), :].astype(_BF)
                vp = v_ref[pl.ds(pstart, BLOCK, stride=dil), :].astype(_BF)
                acc = (jnp.dot(pc.astype(_BF), vc, preferred_element_type=_F32)
                       + jnp.dot(pp.astype(_BF), vp, preferred_element_type=_F32))
                rows = pl.ds(loc, BLOCK, stride=dil)
                acc_s[g, rows, :] = acc
                m_s[g, rows, :] = jnp.broadcast_to(m, acc.shape)
                l_s[g, rows, :] = jnp.broadcast_to(l, acc.shape)
            return carry

        lax.fori_loop(0, tq // (BLOCK * ATTN_UNROLL), units, 0)

    chunk = 256

    def combine(c, carry):
        rows = pl.ds(pl.multiple_of(c * chunk, chunk), chunk)
        ms = [m_s[g, rows, :] for g in range(A_GROUPS)]
        mx = functools.reduce(jnp.maximum, ms)
        ws = [jnp.exp(m - mx) for m in ms]
        num = sum(w * acc_s[g, rows, :] for g, w in enumerate(ws))
        den = sum(w * l_s[g, rows, :] for g, w in enumerate(ws))
        o_ref[rows, :] = (num / den).astype(o_ref.dtype)
        return carry

    lax.fori_loop(0, tq // chunk, combine, 0)


def _attn_a(z, bsz, seq):
    tq = A_TQ
    nh = A_HEADS_PER_GROUP
    kcol = A_QKV // HEAD_DIM
    vcol = 2 * A_QKV // HEAD_DIM

    def qspec(g):
        return pl.BlockSpec((None, tq, HEAD_DIM), lambda b, j, t, g=g: (b, t, g * nh + j))

    def kvspec(g, base):
        return pl.BlockSpec((None, seq, HEAD_DIM), lambda b, j, t, g=g, base=base: (b, 0, base + g * nh + j))

    in_specs = ([qspec(g) for g in range(A_GROUPS)] + [kvspec(g, kcol) for g in range(A_GROUPS)]
                + [kvspec(g, vcol) for g in range(A_GROUPS)])
    return pl.pallas_call(
        _attn_a_kernel,
        grid=(bsz, nh, seq // tq),
        in_specs=in_specs,
        out_specs=pl.BlockSpec((None, tq, HEAD_DIM), lambda b, j, t: (b, t, j)),
        out_shape=jax.ShapeDtypeStruct((bsz, seq, A_OUT), _BF),
        scratch_shapes=[pltpu.VMEM((A_GROUPS, tq, HEAD_DIM), _F32)] * 3,
        compiler_params=_cparams(("parallel", "parallel", "arbitrary")),
        name="attn_dilated",
    )(*([z] * 9))


def _attn_b_kernel(sink_ref, q_ref, k_ref, v_ref, o_ref):
    tq = o_ref.shape[0]
    t0 = pl.program_id(2) * tq
    c = pl.program_id(1)
    sink = jnp.concatenate(
        [jnp.full((BLOCK, 1), sink_ref[c * B_GROUP + g], _F32) for g in range(B_GROUP)], axis=0)

    def unit(u, carry):
        loc = pl.multiple_of(u * BLOCK, BLOCK)
        has_prev = (t0 + loc) > 0
        pstart = pl.multiple_of(jnp.maximum(t0 + loc - BLOCK, 0), BLOCK)
        q4 = q_ref[pl.ds(loc, BLOCK), :].astype(_BF)
        q = jnp.concatenate([q4[:, g * HEAD_DIM:(g + 1) * HEAD_DIM] for g in range(B_GROUP)], axis=0)
        kc = k_ref[pl.ds(pl.multiple_of(t0 + loc, BLOCK), BLOCK), :].astype(_BF)
        vc = v_ref[pl.ds(pl.multiple_of(t0 + loc, BLOCK), BLOCK), :].astype(_BF)
        kp = k_ref[pl.ds(pstart, BLOCK), :].astype(_BF)
        vp = v_ref[pl.ds(pstart, BLOCK), :].astype(_BF)
        sc, sp = _band_scores(q, kc, kp, has_prev, strict_prev=True)
        m = jnp.maximum(jnp.max(jnp.maximum(sc, sp), axis=1, keepdims=True), sink)
        pc = jnp.exp(sc - m)
        pp = jnp.exp(sp - m)
        denom = jnp.sum(pc + pp, axis=1, keepdims=True) + jnp.exp(sink - m)
        acc = (jnp.dot(pc.astype(_BF), vc, preferred_element_type=_F32)
               + jnp.dot(pp.astype(_BF), vp, preferred_element_type=_F32))
        o = (acc / denom).astype(o_ref.dtype)
        for g in range(B_GROUP):
            o_ref[pl.ds(loc, BLOCK), g * HEAD_DIM:(g + 1) * HEAD_DIM] = o[g * BLOCK:(g + 1) * BLOCK, :]
        return carry

    lax.fori_loop(0, tq // BLOCK, unit, 0)


def _attn_b(z, sinks, bsz, seq, tq=1024):
    tq = min(tq, seq)
    gq = B_GROUP * HEAD_DIM
    qcol = 3 * A_QKV // gq
    kcol = (3 * A_QKV + B_Q) // HEAD_DIM
    vcol = (3 * A_QKV + B_Q + B_KV) // HEAD_DIM
    return pl.pallas_call(
        _attn_b_kernel,
        grid=(bsz, B_KV_HEADS, seq // tq),
        in_specs=[
            pl.BlockSpec(memory_space=pltpu.SMEM),
            pl.BlockSpec((None, tq, gq), lambda b, c, t: (b, t, qcol + c)),
            pl.BlockSpec((None, seq, HEAD_DIM), lambda b, c, t: (b, 0, kcol + c)),
            pl.BlockSpec((None, seq, HEAD_DIM), lambda b, c, t: (b, 0, vcol + c)),
        ],
        out_specs=pl.BlockSpec((None, tq, gq), lambda b, c, t: (b, t, c)),
        out_shape=jax.ShapeDtypeStruct((bsz, seq, B_Q), _BF),
        compiler_params=_cparams(("parallel", "parallel", "arbitrary")),
        name="attn_swa_gqa",
    )(sinks, z, z, z)


ROW_GROUP = 128


def _layer_norm(h, g, b):
    mu = jnp.mean(h, axis=-1, keepdims=True)
    var = jnp.mean(jnp.square(h - mu), axis=-1, keepdims=True)
    return (h - mu) * lax.rsqrt(var + LN_EPS) * g + b


def _merge_kernel(oa_ref, ob_ref, ga_ref, gb_ref, x_ref, wa_ref, wb_ref, wo_ref, g_ref, b_ref,
                  h_ref, hb_ref, ht_ref, *, alpha):
    groups = [slice(r, r + ROW_GROUP) for r in range(0, x_ref.shape[0], ROW_GROUP)]

    def branches(rs):
        return (jnp.dot(oa_ref[rs, :], wa_ref[...], preferred_element_type=_F32),
                jnp.dot(ob_ref[rs, :], wb_ref[...], preferred_element_type=_F32))

    def mix(rs, ma, mb):
        merged = jax.nn.sigmoid(ga_ref[rs, :]) * ma + jax.nn.sigmoid(gb_ref[rs, :]) * mb
        return jnp.dot(merged.astype(_BF), wo_ref[...], preferred_element_type=_F32)

    def finish(rs, mixed):
        h = _layer_norm(alpha * x_ref[rs, :] + mixed, g_ref[...], b_ref[...])
        h_ref[rs, :] = h
        hb_ref[rs, :] = h.astype(_BF)
        ht_ref[:, rs] = h.T.astype(_BF)

    first = [branches(rs) for rs in groups]
    pending = None
    for rs, (ma, mb) in zip(groups, first):
        mixed = mix(rs, ma, mb)
        if pending is not None:
            finish(*pending)
        pending = (rs, mixed)
    finish(*pending)


def _merge(oa, ob, z2, x2, wa, wb, wo, g, b, alpha, tm=256):
    t, d = x2.shape
    gcol = (3 * A_QKV + B_Q + 2 * B_KV) // d
    return pl.pallas_call(
        functools.partial(_merge_kernel, alpha=alpha),
        grid=(t // tm,),
        in_specs=[
            pl.BlockSpec((tm, A_OUT), lambda i: (i, 0)),
            pl.BlockSpec((tm, B_Q), lambda i: (i, 0)),
            pl.BlockSpec((tm, d), lambda i: (i, gcol)),
            pl.BlockSpec((tm, d), lambda i: (i, gcol + 1)),
            pl.BlockSpec((tm, d), lambda i: (i, 0)),
            _resident((A_OUT, d), lambda i: (0, 0)),
            _resident((B_Q, d), lambda i: (0, 0)),
            _resident((d, d), lambda i: (0, 0)),
            _resident((1, d), lambda i: (0, 0)),
            _resident((1, d), lambda i: (0, 0)),
        ],
        out_specs=[pl.BlockSpec((tm, d), lambda i: (i, 0)), pl.BlockSpec((tm, d), lambda i: (i, 0)),
                   pl.BlockSpec((d, tm), lambda i: (0, i))],
        out_shape=[jax.ShapeDtypeStruct((t, d), _F32), jax.ShapeDtypeStruct((t, d), _BF),
                   jax.ShapeDtypeStruct((d, t), _BF)],
        compiler_params=_cparams(("parallel",)),
        name="merge_outproj_ln",
    )(oa, ob, z2, z2, x2, wa, wb, wo, g, b)


ROUTE_LANES = 256


def _extract_top(s, k):
    n = s.shape[0]
    iota = lax.broadcasted_iota(jnp.int32, s.shape, 0)
    rank = jnp.full(s.shape, float(k), _F32)
    vals, idxs = [], []
    for kk in range(k):
        m = jnp.max(s, axis=0, keepdims=True)
        idx = jnp.min(jnp.where(s == m, iota, n), axis=0, keepdims=True)
        sel = iota == idx
        rank = jnp.where(sel, float(kk), rank)
        s = jnp.where(sel, -jnp.inf, s)
        vals.append(m)
        idxs.append(idx)
    return jnp.concatenate(vals, axis=0), jnp.concatenate(idxs, axis=0), rank


def _route_exact(s1, s2):
    kk = PEER_TOPK
    a1, _, r1 = _extract_top(s1, kk)
    a2, _, r2 = _extract_top(s2, kk)
    cand = jnp.concatenate([a1[i:i + 1, :] + a2 for i in range(kk)], axis=0)
    best, cidx, _ = _extract_top(cand, kk)
    ci = cidx // kk
    z = jnp.sum(jnp.exp(best - best[0:1, :]), axis=0, keepdims=True)
    jj = jnp.zeros(s1.shape, _F32)
    for i in range(kk):
        cnt = jnp.sum((ci == i).astype(_F32), axis=0, keepdims=True)
        jj = jnp.where(r1 == float(i), cnt, jj)
    return jnp.exp(s1 - a1[0:1, :]) / z, jj, jnp.exp(s2 - a2[0:1, :]), r2


_STRIP_MARK = 2.0 ** 100


def _strip_top(s, k):
    vals = []
    for kk in range(k):
        m = jnp.max(s, axis=0, keepdims=True)
        s = jnp.where(s == m, -_STRIP_MARK * (1.0 + kk / k), s)
        vals.append(m)
    removed = s <= -_STRIP_MARK
    rank = jnp.where(removed, (s * (-1.0 / _STRIP_MARK) - 1.0) * k, float(k))
    return jnp.concatenate(vals, axis=0), removed, rank


_CAND_GROUPS = ((0, 16),) + tuple((i, 8) for i in range(1, 8))


def _route_fast(s1, s2):
    kk = PEER_TOPK
    a1, gone1, _ = _strip_top(s1, kk)
    a2, gone2, r2 = _strip_top(s2, kk)
    pieces = [a1[i:i + 1, :] + a2[0:w, :] for i, w in _CAND_GROUPS] + [a1[8:16, :] + a2[0:1, :]]
    cand = jnp.concatenate(pieces, axis=0)
    _, sel, _ = _strip_top(cand, kk)
    self = sel.astype(_F32)
    z = jnp.sum(jnp.where(sel, jnp.exp(cand - cand[0:1, :]), 0.0), axis=0, keepdims=True)
    rows, off = [], 0
    for _, w in _CAND_GROUPS:
        rows.append(jnp.sum(self[off:off + w, :], axis=0, keepdims=True))
        off += w
    jcnt = jnp.concatenate(rows + [self[off:off + 8, :]], axis=0)
    jj = jnp.zeros(s1.shape, _F32)
    for i in range(kk):
        jj = jnp.where(s1 == a1[i:i + 1, :], jcnt[i:i + 1, :], jj)
    n1 = jnp.sum(gone1.astype(_F32), axis=0, keepdims=True)
    n2 = jnp.sum(gone2.astype(_F32), axis=0, keepdims=True)
    nc = jnp.sum(self, axis=0, keepdims=True)
    bad = jnp.maximum(jnp.maximum(jnp.abs(n1 - kk), jnp.abs(n2 - kk)), jnp.abs(nc - kk))
    return (jnp.exp(s1 - a1[0:1, :]) / z, jj, jnp.exp(s2 - a2[0:1, :]), r2), jnp.max(bad) > 0.0


def _route_kernel(h_ref, wq_ref, sk_ref, p1_ref, jj_ref, e2_ref, r2_ref, q_s):
    tm = h_ref.shape[0]
    q = jnp.dot(h_ref[...], wq_ref[...], preferred_element_type=_F32)
    for hc in range(2 * PEER_HEADS):
        q_s[hc] = q[:, hc * PEER_HALF:(hc + 1) * PEER_HALF].astype(_BF)

    def put(h, lanes, outs):
        p1, jj, e2, r2 = outs
        width = p1.shape[1]
        p1_ref[h, :, lanes] = p1
        jj_ref[h, :, lanes] = jj
        e2_ref[h, :, :, lanes] = e2.astype(_BF).reshape(N_KEYS // 16, 16, width)
        r2_ref[h, :, :, lanes] = r2.astype(_BF).reshape(N_KEYS // 16, 16, width)

    def head(h, carry):
        s1 = lax.dot_general(sk_ref[h, 0], q_s[2 * h], _NT, preferred_element_type=_F32)
        s2 = lax.dot_general(sk_ref[h, 1], q_s[2 * h + 1], _NT, preferred_element_type=_F32)
        tie = None
        for c in range(tm // ROUTE_LANES):
            lanes = slice(c * ROUTE_LANES, (c + 1) * ROUTE_LANES)
            outs, t = _route_fast(s1[:, lanes], s2[:, lanes])
            put(h, lanes, outs)
            tie = t if tie is None else jnp.logical_or(tie, t)

        @pl.when(tie)
        def _():
            put(h, slice(0, tm), _route_exact(s1, s2))
        return carry

    lax.fori_loop(0, PEER_HEADS, head, 0)


def _route(hb, wq, sk, tm=512):
    t, d = hb.shape
    out = jax.ShapeDtypeStruct((PEER_HEADS, N_KEYS, t), _F32)
    ospec = pl.BlockSpec((PEER_HEADS, N_KEYS, tm), lambda i: (0, 0, i))
    outb = jax.ShapeDtypeStruct((PEER_HEADS, N_KEYS // 16, 16, t), _BF)
    obspec = pl.BlockSpec((PEER_HEADS, N_KEYS // 16, 16, tm), lambda i: (0, 0, 0, i))
    return pl.pallas_call(
        _route_kernel,
        grid=(t // tm,),
        in_specs=[
            pl.BlockSpec((tm, d), lambda i: (i, 0)),
            _resident(wq.shape, lambda i: (0, 0)),
            _resident(sk.shape, lambda i: (0, 0, 0, 0)),
        ],
        out_specs=[ospec, ospec, obspec, obspec],
        out_shape=[out, out, outb, outb],
        scratch_shapes=[pltpu.VMEM((2 * PEER_HEADS, tm, PEER_HALF), _BF)],
        compiler_params=_cparams(("parallel",)),
        name="peer_route",
    )(hb, wq, sk)


EXPERT_SUB = 256


def _gform(b, p1_ref, jj_ref, e2_ref, r2_ref):
    tm = jj_ref.shape[2]
    gsum = None
    for h in range(PEER_HEADS):
        jrow = jnp.broadcast_to(jj_ref[h, b:b + 1, :], (16, tm)).astype(_BF)[None]
        prow = jnp.broadcast_to(p1_ref[h, b:b + 1, :], (16, tm)).astype(_BF)[None]
        term = jnp.where(r2_ref[h] < jrow, e2_ref[h], jnp.zeros((), _BF)) * prow
        gsum = term if gsum is None else gsum + term
    return gsum


def _experts_kernel(ht_ref, u_ref, vt_ref, p1_ref, jj_ref, e2_ref, r2_ref, o_ref, acc_s, gelu_s, w_s):
    e = pl.program_id(1)
    eb, tm = gelu_s.shape

    @pl.when(e == 0)
    def _():
        def zero(r, c):
            acc_s[pl.ds(pl.multiple_of(r * 64, 64), 64), :] = jnp.zeros((64, tm), _F32)
            return c
        lax.fori_loop(0, acc_s.shape[0] // 64, zero, 0)

    for a in range(eb // EXPERT_SUB):
        rows = slice(a * EXPERT_SUB, (a + 1) * EXPERT_SUB)
        x = jnp.dot(u_ref[rows, :], ht_ref[...], preferred_element_type=_F32)
        gelu_s[rows, :] = (0.5 * x * (1.0 + lax.erf(x * np.float32(np.sqrt(0.5))))).astype(_BF)
    for b in range(eb // N_KEYS):
        rows = slice(b * N_KEYS, (b + 1) * N_KEYS)
        g = _gform(b, p1_ref, jj_ref, e2_ref, r2_ref)
        w_s[rows, :] = (g * gelu_s[rows, :].reshape(N_KEYS // 16, 16, tm)).reshape(N_KEYS, tm)
    acc_s[...] += jnp.dot(vt_ref[...], w_s[...], preferred_element_type=_F32)

    @pl.when(e == pl.num_programs(1) - 1)
    def _():
        o_ref[...] = acc_s[...].T


def _experts(htb, ub, vtb, p1, jj, e2, r2, tm=512, eb=2048):
    d, t = htb.shape
    ne = ub.shape[0]
    rspec = pl.BlockSpec((PEER_HEADS, eb // N_KEYS, tm), lambda i, e: (0, e, i))
    bspec = pl.BlockSpec((PEER_HEADS, N_KEYS // 16, 16, tm), lambda i, e: (0, 0, 0, i), pipeline_mode=pl.Buffered(1))
    return pl.pallas_call(
        _experts_kernel,
        grid=(t // tm, ne // eb),
        in_specs=[
            pl.BlockSpec((d, tm), lambda i, e: (0, i), pipeline_mode=pl.Buffered(1)),
            pl.BlockSpec((eb, d), lambda i, e: (e, 0)),
            pl.BlockSpec((d, eb), lambda i, e: (0, e)),
            rspec, rspec, bspec, bspec,
        ],
        out_specs=pl.BlockSpec((tm, d), lambda i, e: (i, 0)),
        out_shape=jax.ShapeDtypeStruct((t, d), _F32),
        scratch_shapes=[pltpu.VMEM((d, tm), _F32), pltpu.VMEM((eb, tm), _BF), pltpu.VMEM((eb, tm), _BF)],
        compiler_params=_cparams(("parallel", "arbitrary")),
        name="peer_experts",
    )(htb, ub, vtb, p1, jj, e2, r2)


def _final_kernel(h_ref, hb_ref, f_ref, p_ref, wg_ref, wp_ref, g_ref, b_ref, o_ref, *, alpha):
    groups = [slice(r, r + ROW_GROUP) for r in range(0, h_ref.shape[0], ROW_GROUP)]
    dots = [(jnp.dot(hb_ref[rs, :], wg_ref[...], preferred_element_type=_F32),
             jnp.dot(p_ref[rs, :], wp_ref[...], preferred_element_type=_F32)) for rs in groups]
    for rs, (gate, proj) in zip(groups, dots):
        ple = jax.nn.sigmoid(gate) * proj
        o_ref[rs, :] = _layer_norm(alpha * h_ref[rs, :] + f_ref[rs, :] + ple, g_ref[...], b_ref[...])


def _final(h, hb, ffn, pb, wg, wp, g, b, alpha, tm=256):
    t, d = h.shape
    pd = pb.shape[1]
    return pl.pallas_call(
        functools.partial(_final_kernel, alpha=alpha),
        grid=(t // tm,),
        in_specs=[
            pl.BlockSpec((tm, d), lambda i: (i, 0)),
            pl.BlockSpec((tm, d), lambda i: (i, 0)),
            pl.BlockSpec((tm, d), lambda i: (i, 0)),
            pl.BlockSpec((tm, pd), lambda i: (i, 0)),
            _resident((d, d), lambda i: (0, 0)),
            _resident((pd, d), lambda i: (0, 0)),
            _resident((1, d), lambda i: (0, 0)),
            _resident((1, d), lambda i: (0, 0)),
        ],
        out_specs=pl.BlockSpec((tm, d), lambda i: (i, 0)),
        out_shape=jax.ShapeDtypeStruct((t, d), _F32),
        compiler_params=_cparams(("parallel",)),
        name="ple_final_ln",
    )(h, hb, ffn, pb, wg, wp, g, b)


def _rope_tables(seq):
    inv = 1.0 / (ROPE_THETA ** (jnp.arange(0, HEAD_DIM, 2, dtype=_F32) / HEAD_DIM))
    ang = jnp.arange(seq, dtype=_F32)[:, None] * inv[None, :]
    cos, sin = jnp.cos(ang), jnp.sin(ang)
    return jnp.concatenate([cos, cos], axis=1), jnp.concatenate([-sin, sin], axis=1)


def kernel(x, p, w_in, sinks, w_branch_a, w_branch_b, w_out, ln1_g, ln1_b, peer_wq, peer_subkeys, peer_u,
           peer_v, ple_gate, ple_proj, ln2_g, ln2_b):
    bsz, seq, d = x.shape
    depth = w_in.shape[0]
    t = bsz * seq
    alpha = (2 * depth) ** 0.25
    assert seq % A_TQ == 0 and w_in.shape[2] == 3 * A_QKV + B_Q + 2 * B_KV + 2 * d
    cos2, sin2 = _rope_tables(seq)
    rope_cols = ((0, 2 * A_QKV), (3 * A_QKV, 3 * A_QKV + B_Q + B_KV))
    h = x.reshape(t, d)
    for i in range(depth):
        z = _inproj(h, w_in[i].astype(_BF), cos2, sin2, seq, rope_cols)
        z3 = z.reshape(bsz, seq, -1)
        oa = _attn_a(z3, bsz, seq).reshape(t, A_OUT)
        ob = _attn_b(z3, sinks[i].astype(_F32), bsz, seq).reshape(t, B_Q)
        h1, h1b, h1t = _merge(oa, ob, z, h, w_branch_a[i].astype(_BF), w_branch_b[i].astype(_BF),
                              w_out[i].astype(_BF), ln1_g[i].reshape(1, d), ln1_b[i].reshape(1, d), alpha)
        p1, jj, e2, r2 = _route(h1b, peer_wq[i].astype(_BF), peer_subkeys[i].astype(_BF))
        ffn = _experts(h1t, peer_u[i].astype(_BF), peer_v[i].T.astype(_BF), p1, jj, e2, r2)
        h = _final(h1, h1b, ffn, p[i].reshape(t, -1).astype(_BF), ple_gate[i].astype(_BF),
                   ple_proj[i].astype(_BF), ln2_g[i].reshape(1, d), ln2_b[i].reshape(1, d), alpha)
    return h.reshape(bsz, seq, d)
```

```python
import functools

import numpy as np
import jax
import jax.numpy as jnp
from jax import lax
from jax.experimental import pallas as pl
from jax.experimental.pallas import tpu as pltpu

HEAD_DIM = 128
ROPE_THETA = 10000.0
BLOCK = 128
A_PATTERNS = ((128, 1), (512, 4), (2048, 16))
A_HEADS_PER_GROUP = 4
A_GROUPS = len(A_PATTERNS)
A_HEADS = A_HEADS_PER_GROUP * A_GROUPS
A_QKV = A_HEADS * HEAD_DIM
A_OUT = A_HEADS_PER_GROUP * HEAD_DIM
B_Q_HEADS = 8
B_KV_HEADS = 2
B_GROUP = B_Q_HEADS // B_KV_HEADS
B_Q = B_Q_HEADS * HEAD_DIM
B_KV = B_KV_HEADS * HEAD_DIM
PEER_HEADS = 8
PEER_HALF = 128
N_KEYS = 128
PEER_TOPK = 16
LN_EPS = 1e-5

V7X_VMEM_LIMIT = 60 * 1024 * 1024
A_TQ = max(d for _, d in A_PATTERNS) * BLOCK

ATTN_UNROLL = 8
B_UNROLL = 1

_NT = (((1,), (1,)), ((), ()))
_BF = jnp.bfloat16
_F32 = jnp.float32


def _cparams(sem):
    return pltpu.CompilerParams(dimension_semantics=sem, vmem_limit_bytes=V7X_VMEM_LIMIT)


def _resident(block_shape, index_map):
    return pl.BlockSpec(block_shape, index_map, pipeline_mode=pl.Buffered(1))


INPROJ_ROWS = 128


def _inproj_kernel(x_ref, w_ref, cos_ref, sin_ref, o_ref, xb_s, *, patterns):
    j = pl.program_id(1)
    tm = x_ref.shape[0]

    @pl.when(j == 0)
    def _():
        def cast(r, c):
            rows = pl.ds(pl.multiple_of(r * BLOCK, BLOCK), BLOCK)
            xb_s[rows, :] = x_ref[rows, :].astype(_BF)
            return c
        lax.fori_loop(0, tm // BLOCK, cast, 0)

    def matmul(r):
        rows = slice(r * INPROJ_ROWS, (r + 1) * INPROJ_ROWS)
        return jnp.dot(xb_s[rows, :], w_ref[...], preferred_element_type=_F32)

    def epilogue(r, acc):
        rows = slice(r * INPROJ_ROWS, (r + 1) * INPROJ_ROWS)
        for c in range(acc.shape[1] // HEAD_DIM):
            t = acc[:, c * HEAD_DIM:(c + 1) * HEAD_DIM]
            blocks = [jb for jb, pat in enumerate(patterns) if pat[c]]
            if blocks:
                flag = functools.reduce(jnp.logical_or, [j == jb for jb in blocks])
                rope = t * cos_ref[rows, :] + pltpu.roll(t, HEAD_DIM // 2, 1) * sin_ref[rows, :]
                t = jnp.where(flag, rope, t)
            o_ref[rows, c * HEAD_DIM:(c + 1) * HEAD_DIM] = t

    nr = tm // INPROJ_ROWS
    acc = matmul(0)
    for r in range(nr):
        nxt = matmul(r + 1) if r + 1 < nr else None
        epilogue(r, acc)
        acc = nxt


def _inproj(x, wb, cos2, sin2, seq, rope_cols, tm=1024, tn=1024):
    m, k = x.shape
    n = wb.shape[1]
    tm = min(tm, seq)
    tn = tn if n % tn == 0 else tn // 2
    assert n % tn == 0 and m % tm == 0
    patterns = tuple(
        tuple(any(lo <= (jb * tn + c * HEAD_DIM) < hi for lo, hi in rope_cols) for c in range(tn // HEAD_DIM))
        for jb in range(n // tn))
    sblocks = seq // tm
    return pl.pallas_call(
        functools.partial(_inproj_kernel, patterns=patterns),
        grid=(m // tm, n // tn),
        in_specs=[
            pl.BlockSpec((tm, k), lambda i, j: (i, 0)),
            pl.BlockSpec((k, tn), lambda i, j: (0, j)),
            pl.BlockSpec((tm, HEAD_DIM), lambda i, j: (i % sblocks, 0)),
            pl.BlockSpec((tm, HEAD_DIM), lambda i, j: (i % sblocks, 0)),
        ],
        out_specs=pl.BlockSpec((tm, tn), lambda i, j: (i, j)),
        out_shape=jax.ShapeDtypeStruct((m, n), _F32),
        scratch_shapes=[pltpu.VMEM((tm, k), _BF)],
        compiler_params=_cparams(("parallel", "arbitrary")),
        name="inproj_rope",
    )(x, wb, cos2, sin2)


def _band_scores(q, kc, kp, has_prev, strict_prev):
    scale = HEAD_DIM ** -0.5
    sc = lax.dot_general(q, kc, _NT, preferred_element_type=_F32) * scale
    sp = lax.dot_general(q, kp, _NT, preferred_element_type=_F32) * scale
    row = lax.broadcasted_iota(jnp.int32, sc.shape, 0) % BLOCK
    col = lax.broadcasted_iota(jnp.int32, sc.shape, 1)
    prev_ok = (col > row) if strict_prev else (col >= row)
    sc = jnp.where(col <= row, sc, -jnp.inf)
    sp = jnp.where(jnp.logical_and(prev_ok, has_prev), sp, -jnp.inf)
    return sc, sp


def _attn_a_kernel(q0, q1, q2, k0, k1, k2, v0, v1, v2, o_ref, acc_s, m_s, l_s):
    tq = o_ref.shape[0]
    t0 = pl.program_id(2) * tq
    refs = ((q0, k0, v0), (q1, k1, v1), (q2, k2, v2))
    for g, (window, dil) in enumerate(A_PATTERNS):
        q_ref, k_ref, v_ref = refs[g]
        span = BLOCK * dil

        def units(ug, carry, q_ref=q_ref, k_ref=k_ref, v_ref=v_ref, dil=dil, span=span, g=g):
            us = [ug * ATTN_UNROLL + i for i in range(ATTN_UNROLL)]
            locs = [(u // dil) * span + u % dil for u in us]
            scores = []
            for u, loc in zip(us, locs):
                has_prev = (t0 + (u // dil) * span) > 0
                pstart = jnp.maximum(t0 + loc - span, u % dil)
                q = q_ref[pl.ds(loc, BLOCK, stride=dil), :].astype(_BF)
                kc = k_ref[pl.ds(t0 + loc, BLOCK, stride=dil), :].astype(_BF)
                kp = k_ref[pl.ds(pstart, BLOCK, stride=dil), :].astype(_BF)
                scores.append(_band_scores(q, kc, kp, has_prev, strict_prev=False) + (pstart,))
            ms = [jnp.max(jnp.maximum(sc, sp), axis=1, keepdims=True) for sc, sp, _ in scores]
            probs = [(jnp.exp(sc - m), jnp.exp(sp - m)) for (sc, sp, _), m in zip(scores, ms)]
            ls = [jnp.sum(pc + pp, axis=1, keepdims=True) for pc, pp in probs]
            for loc, (_, _, pstart), m, (pc, pp), l in zip(locs, scores, ms, probs, ls):
                vc = v_ref[pl.ds(t0 + loc, BLOCK, stride=dil), :].astype(_BF)
                vp = v_ref[pl.ds(pstart, BLOCK, stride=dil), :].astype(_BF)
                acc = (jnp.dot(pc.astype(_BF), vc, preferred_element_type=_F32)
                       + jnp.dot(pp.astype(_BF), vp, preferred_element_type=_F32))
                rows = pl.ds(loc, BLOCK, stride=dil)
                acc_s[g, rows, :] = acc
                m_s[g, rows, :] = jnp.broadcast_to(m, acc.shape)
                l_s[g, rows, :] = jnp.broadcast_to(l, acc.shape)
            return carry

        lax.fori_loop(0, tq // (BLOCK * ATTN_UNROLL), units, 0)

    chunk = 256

    def combine(c, carry):
        rows = pl.ds(pl.multiple_of(c * chunk, chunk), chunk)
        ms = [m_s[g, rows, :] for g in range(A_GROUPS)]
        mx = functools.reduce(jnp.maximum, ms)
        ws = [jnp.exp(m - mx) for m in ms]
        num = sum(w * acc_s[g, rows, :] for g, w in enumerate(ws))
        den = sum(w * l_s[g, rows, :] for g, w in enumerate(ws))
        o_ref[rows, :] = (num / den).astype(o_ref.dtype)
        return carry

    lax.fori_loop(0, tq // chunk, combine, 0)


def _attn_a(z, bsz, seq):
    tq = A_TQ
    nh = A_HEADS_PER_GROUP
    kcol = A_QKV // HEAD_DIM
    vcol = 2 * A_QKV // HEAD_DIM

    def qspec(g):
        return pl.BlockSpec((None, tq, HEAD_DIM), lambda b, j, t, g=g: (b, t, g * nh + j))

    def kvspec(g, base):
        return pl.BlockSpec((None, seq, HEAD_DIM), lambda b, j, t, g=g, base=base: (b, 0, base + g * nh + j))

    in_specs = ([qspec(g) for g in range(A_GROUPS)] + [kvspec(g, kcol) for g in range(A_GROUPS)]
                + [kvspec(g, vcol) for g in range(A_GROUPS)])
    return pl.pallas_call(
        _attn_a_kernel,
        grid=(bsz, nh, seq // tq),
        in_specs=in_specs,
        out_specs=pl.BlockSpec((None, tq, HEAD_DIM), lambda b, j, t: (b, t, j)),
        out_shape=jax.ShapeDtypeStruct((bsz, seq, A_OUT), _BF),
        scratch_shapes=[pltpu.VMEM((A_GROUPS, tq, HEAD_DIM), _F32)] * 3,
        compiler_params=_cparams(("parallel", "parallel", "arbitrary")),
        name="attn_dilated",
    )(*([z] * 9))


def _attn_b_kernel(sink_ref, q_ref, k_ref, v_ref, o_ref):
    tq = o_ref.shape[0]
    t0 = pl.program_id(2) * tq
    c = pl.program_id(1)
    sink = jnp.concatenate(
        [jnp.full((BLOCK, 1), sink_ref[c * B_GROUP + g], _F32) for g in range(B_GROUP)], axis=0)

    def units(ug, carry):
        locs = [pl.multiple_of((ug * B_UNROLL + i) * BLOCK, BLOCK) for i in range(B_UNROLL)]
        scores = []
        for loc in locs:
            has_prev = (t0 + loc) > 0
            pstart = pl.multiple_of(jnp.maximum(t0 + loc - BLOCK, 0), BLOCK)
            q4 = q_ref[pl.ds(loc, BLOCK), :].astype(_BF)
            q = jnp.concatenate([q4[:, g * HEAD_DIM:(g + 1) * HEAD_DIM] for g in range(B_GROUP)], axis=0)
            kc = k_ref[pl.ds(pl.multiple_of(t0 + loc, BLOCK), BLOCK), :].astype(_BF)
            kp = k_ref[pl.ds(pstart, BLOCK), :].astype(_BF)
            scores.append(_band_scores(q, kc, kp, has_prev, strict_prev=True) + (pstart,))
        ms = [jnp.maximum(jnp.max(jnp.maximum(sc, sp), axis=1, keepdims=True), sink) for sc, sp, _ in scores]
        probs = [(jnp.exp(sc - m), jnp.exp(sp - m)) for (sc, sp, _), m in zip(scores, ms)]
        denoms = [jnp.sum(pc + pp, axis=1, keepdims=True) + jnp.exp(sink - m) for (pc, pp), m in zip(probs, ms)]
        for loc, (_, _, pstart), (pc, pp), denom in zip(locs, scores, probs, denoms):
            vc = v_ref[pl.ds(pl.multiple_of(t0 + loc, BLOCK), BLOCK), :].astype(_BF)
            vp = v_ref[pl.ds(pstart, BLOCK), :].astype(_BF)
            acc = (jnp.dot(pc.astype(_BF), vc, preferred_element_type=_F32)
                   + jnp.dot(pp.astype(_BF), vp, preferred_element_type=_F32))
            o = (acc / denom).astype(o_ref.dtype)
            for g in range(B_GROUP):
                o_ref[pl.ds(loc, BLOCK), g * HEAD_DIM:(g + 1) * HEAD_DIM] = o[g * BLOCK:(g + 1) * BLOCK, :]
        return carry

    lax.fori_loop(0, tq // (BLOCK * B_UNROLL), units, 0)


def _attn_b(z, sinks, bsz, seq, tq=1024):
    tq = min(tq, seq)
    gq = B_GROUP * HEAD_DIM
    qcol = 3 * A_QKV // gq
    kcol = (3 * A_QKV + B_Q) // HEAD_DIM
    vcol = (3 * A_QKV + B_Q + B_KV) // HEAD_DIM
    return pl.pallas_call(
        _attn_b_kernel,
        grid=(bsz, B_KV_HEADS, seq // tq),
        in_specs=[
            pl.BlockSpec(memory_space=pltpu.SMEM),
            pl.BlockSpec((None, tq, gq), lambda b, c, t: (b, t, qcol + c)),
            pl.BlockSpec((None, seq, HEAD_DIM), lambda b, c, t: (b, 0, kcol + c)),
            pl.BlockSpec((None, seq, HEAD_DIM), lambda b, c, t: (b, 0, vcol + c)),
        ],
        out_specs=pl.BlockSpec((None, tq, gq), lambda b, c, t: (b, t, c)),
        out_shape=jax.ShapeDtypeStruct((bsz, seq, B_Q), _BF),
        compiler_params=_cparams(("parallel", "parallel", "arbitrary")),
        name="attn_swa_gqa",
    )(sinks, z, z, z)


ROW_GROUP = 128


def _layer_norm(h, g, b):
    mu = jnp.mean(h, axis=-1, keepdims=True)
    var = jnp.mean(jnp.square(h - mu), axis=-1, keepdims=True)
    return (h - mu) * lax.rsqrt(var + LN_EPS) * g + b


def _merge_kernel(oa_ref, ob_ref, ga_ref, gb_ref, x_ref, wa_ref, wb_ref, wo_ref, g_ref, b_ref,
                  h_ref, hb_ref, ht_ref, *, alpha):
    groups = [slice(r, r + ROW_GROUP) for r in range(0, x_ref.shape[0], ROW_GROUP)]

    def branches(rs):
        return (jnp.dot(oa_ref[rs, :], wa_ref[...], preferred_element_type=_F32),
                jnp.dot(ob_ref[rs, :], wb_ref[...], preferred_element_type=_F32))

    def mix(rs, ma, mb):
        merged = jax.nn.sigmoid(ga_ref[rs, :]) * ma + jax.nn.sigmoid(gb_ref[rs, :]) * mb
        return jnp.dot(merged.astype(_BF), wo_ref[...], preferred_element_type=_F32)

    def finish(rs, mixed):
        h = _layer_norm(alpha * x_ref[rs, :] + mixed, g_ref[...], b_ref[...])
        h_ref[rs, :] = h
        hb_ref[rs, :] = h.astype(_BF)
        ht_ref[:, rs] = h.T.astype(_BF)

    first = [branches(rs) for rs in groups]
    pending = None
    for rs, (ma, mb) in zip(groups, first):
        mixed = mix(rs, ma, mb)
        if pending is not None:
            finish(*pending)
        pending = (rs, mixed)
    finish(*pending)


def _merge(oa, ob, z2, x2, wa, wb, wo, g, b, alpha, tm=256):
    t, d = x2.shape
    gcol = (3 * A_QKV + B_Q + 2 * B_KV) // d
    return pl.pallas_call(
        functools.partial(_merge_kernel, alpha=alpha),
        grid=(t // tm,),
        in_specs=[
            pl.BlockSpec((tm, A_OUT), lambda i: (i, 0)),
            pl.BlockSpec((tm, B_Q), lambda i: (i, 0)),
            pl.BlockSpec((tm, d), lambda i: (i, gcol)),
            pl.BlockSpec((tm, d), lambda i: (i, gcol + 1)),
            pl.BlockSpec((tm, d), lambda i: (i, 0)),
            _resident((A_OUT, d), lambda i: (0, 0)),
            _resident((B_Q, d), lambda i: (0, 0)),
            _resident((d, d), lambda i: (0, 0)),
            _resident((1, d), lambda i: (0, 0)),
            _resident((1, d), lambda i: (0, 0)),
        ],
        out_specs=[pl.BlockSpec((tm, d), lambda i: (i, 0)), pl.BlockSpec((tm, d), lambda i: (i, 0)),
                   pl.BlockSpec((d, tm), lambda i: (0, i))],
        out_shape=[jax.ShapeDtypeStruct((t, d), _F32), jax.ShapeDtypeStruct((t, d), _BF),
                   jax.ShapeDtypeStruct((d, t), _BF)],
        compiler_params=_cparams(("parallel",)),
        name="merge_outproj_ln",
    )(oa, ob, z2, z2, x2, wa, wb, wo, g, b)


ROUTE_LANES = 256


def _extract_top(s, k):
    n = s.shape[0]
    iota = lax.broadcasted_iota(jnp.int32, s.shape, 0)
    rank = jnp.full(s.shape, float(k), _F32)
    vals, idxs = [], []
    for kk in range(k):
        m = jnp.max(s, axis=0, keepdims=True)
        idx = jnp.min(jnp.where(s == m, iota, n), axis=0, keepdims=True)
        sel = iota == idx
        rank = jnp.where(sel, float(kk), rank)
        s = jnp.where(sel, -jnp.inf, s)
        vals.append(m)
        idxs.append(idx)
    return jnp.concatenate(vals, axis=0), jnp.concatenate(idxs, axis=0), rank


def _route_exact(s1, s2):
    kk = PEER_TOPK
    a1, _, r1 = _extract_top(s1, kk)
    a2, _, r2 = _extract_top(s2, kk)
    cand = jnp.concatenate([a1[i:i + 1, :] + a2 for i in range(kk)], axis=0)
    best, cidx, _ = _extract_top(cand, kk)
    ci = cidx // kk
    z = jnp.sum(jnp.exp(best - best[0:1, :]), axis=0, keepdims=True)
    jj = jnp.zeros(s1.shape, _F32)
    for i in range(kk):
        cnt = jnp.sum((ci == i).astype(_F32), axis=0, keepdims=True)
        jj = jnp.where(r1 == float(i), cnt, jj)
    return jnp.exp(s1 - a1[0:1, :]) / z, jj, jnp.exp(s2 - a2[0:1, :]), r2


_STRIP_MARK = 2.0 ** 100


def _strip_top(s, k):
    vals = []
    for kk in range(k):
        m = jnp.max(s, axis=0, keepdims=True)
        s = jnp.where(s == m, -_STRIP_MARK * (1.0 + kk / k), s)
        vals.append(m)
    removed = s <= -_STRIP_MARK
    rank = jnp.where(removed, (s * (-1.0 / _STRIP_MARK) - 1.0) * k, float(k))
    return jnp.concatenate(vals, axis=0), removed, rank


_CAND_GROUPS = ((0, 16),) + tuple((i, 8) for i in range(1, 8))


def _route_fast(s1, s2):
    kk = PEER_TOPK
    a1, gone1, _ = _strip_top(s1, kk)
    a2, gone2, r2 = _strip_top(s2, kk)
    pieces = [a1[i:i + 1, :] + a2[0:w, :] for i, w in _CAND_GROUPS] + [a1[8:16, :] + a2[0:1, :]]
    cand = jnp.concatenate(pieces, axis=0)
    _, sel, _ = _strip_top(cand, kk)
    self = sel.astype(_F32)
    z = jnp.sum(jnp.where(sel, jnp.exp(cand - cand[0:1, :]), 0.0), axis=0, keepdims=True)
    rows, off = [], 0
    for _, w in _CAND_GROUPS:
        rows.append(jnp.sum(self[off:off + w, :], axis=0, keepdims=True))
        off += w
    jcnt = jnp.concatenate(rows + [self[off:off + 8, :]], axis=0)
    jj = jnp.zeros(s1.shape, _F32)
    for i in range(kk):
        jj = jnp.where(s1 == a1[i:i + 1, :], jcnt[i:i + 1, :], jj)
    n1 = jnp.sum(gone1.astype(_F32), axis=0, keepdims=True)
    n2 = jnp.sum(gone2.astype(_F32), axis=0, keepdims=True)
    nc = jnp.sum(self, axis=0, keepdims=True)
    bad = jnp.maximum(jnp.maximum(jnp.abs(n1 - kk), jnp.abs(n2 - kk)), jnp.abs(nc - kk))
    return (jnp.exp(s1 - a1[0:1, :]) / z, jj, jnp.exp(s2 - a2[0:1, :]), r2), jnp.max(bad) > 0.0


def _route_kernel(h_ref, wq_ref, sk_ref, p1_ref, jj_ref, e2_ref, r2_ref, q_s):
    tm = h_ref.shape[0]

    def project(h):
        q = jnp.dot(h_ref[...], wq_ref[h], preferred_element_type=_F32)
        q_s[2 * h] = q[:, :PEER_HALF].astype(_BF)
        q_s[2 * h + 1] = q[:, PEER_HALF:].astype(_BF)

    def put(h, lanes, outs):
        p1, jj, e2, r2 = outs
        width = p1.shape[1]
        p1_ref[h, :, lanes] = p1
        jj_ref[h, :, lanes] = jj
        e2_ref[h, :, :, lanes] = e2.astype(_BF).reshape(N_KEYS // 16, 16, width)
        r2_ref[h, :, :, lanes] = r2.astype(_BF).reshape(N_KEYS // 16, 16, width)

    def head(h, carry):
        s1 = lax.dot_general(sk_ref[h, 0], q_s[2 * h], _NT, preferred_element_type=_F32)
        s2 = lax.dot_general(sk_ref[h, 1], q_s[2 * h + 1], _NT, preferred_element_type=_F32)
        project(jnp.minimum(h + 1, PEER_HEADS - 1))
        tie = None
        for c in range(tm // ROUTE_LANES):
            lanes = slice(c * ROUTE_LANES, (c + 1) * ROUTE_LANES)
            outs, t = _route_fast(s1[:, lanes], s2[:, lanes])
            put(h, lanes, outs)
            tie = t if tie is None else jnp.logical_or(tie, t)

        @pl.when(tie)
        def _():
            put(h, slice(0, tm), _route_exact(s1, s2))
        return carry

    project(0)
    lax.fori_loop(0, PEER_HEADS, head, 0)


def _route(hb, wq, sk, tm=512):
    t, d = hb.shape
    out = jax.ShapeDtypeStruct((PEER_HEADS, N_KEYS, t), _F32)
    ospec = pl.BlockSpec((PEER_HEADS, N_KEYS, tm), lambda i: (0, 0, i))
    outb = jax.ShapeDtypeStruct((PEER_HEADS, N_KEYS // 16, 16, t), _BF)
    obspec = pl.BlockSpec((PEER_HEADS, N_KEYS // 16, 16, tm), lambda i: (0, 0, 0, i))
    return pl.pallas_call(
        _route_kernel,
        grid=(t // tm,),
        in_specs=[
            pl.BlockSpec((tm, d), lambda i: (i, 0)),
            _resident(wq.shape, lambda i: (0, 0, 0)),
            _resident(sk.shape, lambda i: (0, 0, 0, 0)),
        ],
        out_specs=[ospec, ospec, obspec, obspec],
        out_shape=[out, out, outb, outb],
        scratch_shapes=[pltpu.VMEM((2 * PEER_HEADS, tm, PEER_HALF), _BF)],
        compiler_params=_cparams(("parallel",)),
        name="peer_route",
    )(hb, wq, sk)


EXPERT_SUB = 256


def _gform(b, p1_ref, jj_ref, e2_ref, r2_ref):
    tm = jj_ref.shape[2]
    gsum = None
    for h in range(PEER_HEADS):
        jrow = jnp.broadcast_to(jj_ref[h, b:b + 1, :], (16, tm)).astype(_BF)[None]
        prow = jnp.broadcast_to(p1_ref[h, b:b + 1, :], (16, tm)).astype(_BF)[None]
        term = jnp.where(r2_ref[h] < jrow, e2_ref[h], jnp.zeros((), _BF)) * prow
        gsum = term if gsum is None else gsum + term
    return gsum


def _experts_kernel(ht_ref, u_ref, vt_ref, p1_ref, jj_ref, e2_ref, r2_ref, o_ref, acc_s, gelu_s, w_s):
    e = pl.program_id(1)
    eb, tm = gelu_s.shape

    @pl.when(e == 0)
    def _():
        def zero(r, c):
            acc_s[pl.ds(pl.multiple_of(r * 64, 64), 64), :] = jnp.zeros((64, tm), _F32)
            return c
        lax.fori_loop(0, acc_s.shape[0] // 64, zero, 0)

    for a in range(eb // EXPERT_SUB):
        rows = slice(a * EXPERT_SUB, (a + 1) * EXPERT_SUB)
        x = jnp.dot(u_ref[rows, :], ht_ref[...], preferred_element_type=_F32)
        gelu_s[rows, :] = (0.5 * x * (1.0 + lax.erf(x * np.float32(np.sqrt(0.5))))).astype(_BF)
    for b in range(eb // N_KEYS):
        rows = slice(b * N_KEYS, (b + 1) * N_KEYS)
        g = _gform(b, p1_ref, jj_ref, e2_ref, r2_ref)
        w_s[rows, :] = (g * gelu_s[rows, :].reshape(N_KEYS // 16, 16, tm)).reshape(N_KEYS, tm)
    acc_s[...] += jnp.dot(vt_ref[...], w_s[...], preferred_element_type=_F32)

    @pl.when(e == pl.num_programs(1) - 1)
    def _():
        o_ref[...] = acc_s[...].T


def _experts(htb, ub, vtb, p1, jj, e2, r2, tm=512, eb=2048):
    d, t = htb.shape
    ne = ub.shape[0]
    rspec = pl.BlockSpec((PEER_HEADS, eb // N_KEYS, tm), lambda i, e: (0, e, i))
    bspec = pl.BlockSpec((PEER_HEADS, N_KEYS // 16, 16, tm), lambda i, e: (0, 0, 0, i), pipeline_mode=pl.Buffered(1))
    return pl.pallas_call(
        _experts_kernel,
        grid=(t // tm, ne // eb),
        in_specs=[
            pl.BlockSpec((d, tm), lambda i, e: (0, i), pipeline_mode=pl.Buffered(1)),
            pl.BlockSpec((eb, d), lambda i, e: (e, 0)),
            pl.BlockSpec((d, eb), lambda i, e: (0, e)),
            rspec, rspec, bspec, bspec,
        ],
        out_specs=pl.BlockSpec((tm, d), lambda i, e: (i, 0)),
        out_shape=jax.ShapeDtypeStruct((t, d), _F32),
        scratch_shapes=[pltpu.VMEM((d, tm), _F32), pltpu.VMEM((eb, tm), _BF), pltpu.VMEM((eb, tm), _BF)],
        compiler_params=_cparams(("parallel", "arbitrary")),
        name="peer_experts",
    )(htb, ub, vtb, p1, jj, e2, r2)


def _final_kernel(h_ref, hb_ref, f_ref, p_ref, wg_ref, wp_ref, g_ref, b_ref, o_ref, *, alpha):
    groups = [slice(r, r + ROW_GROUP) for r in range(0, h_ref.shape[0], ROW_GROUP)]
    dots = [(jnp.dot(hb_ref[rs, :], wg_ref[...], preferred_element_type=_F32),
             jnp.dot(p_ref[rs, :], wp_ref[...], preferred_element_type=_F32)) for rs in groups]
    for rs, (gate, proj) in zip(groups, dots):
        ple = jax.nn.sigmoid(gate) * proj
        o_ref[rs, :] = _layer_norm(alpha * h_ref[rs, :] + f_ref[rs, :] + ple, g_ref[...], b_ref[...])


def _final(h, hb, ffn, pb, wg, wp, g, b, alpha, tm=256):
    t, d = h.shape
    pd = pb.shape[1]
    return pl.pallas_call(
        functools.partial(_final_kernel, alpha=alpha),
        grid=(t // tm,),
        in_specs=[
            pl.BlockSpec((tm, d), lambda i: (i, 0)),
            pl.BlockSpec((tm, d), lambda i: (i, 0)),
            pl.BlockSpec((tm, d), lambda i: (i, 0)),
            pl.BlockSpec((tm, pd), lambda i: (i, 0)),
            _resident((d, d), lambda i: (0, 0)),
            _resident((pd, d), lambda i: (0, 0)),
            _resident((1, d), lambda i: (0, 0)),
            _resident((1, d), lambda i: (0, 0)),
        ],
        out_specs=pl.BlockSpec((tm, d), lambda i: (i, 0)),
        out_shape=jax.ShapeDtypeStruct((t, d), _F32),
        compiler_params=_cparams(("parallel",)),
        name="ple_final_ln",
    )(h, hb, ffn, pb, wg, wp, g, b)


def _rope_tables(seq):
    inv = 1.0 / (ROPE_THETA ** (jnp.arange(0, HEAD_DIM, 2, dtype=_F32) / HEAD_DIM))
    ang = jnp.arange(seq, dtype=_F32)[:, None] * inv[None, :]
    cos, sin = jnp.cos(ang), jnp.sin(ang)
    return jnp.concatenate([cos, cos], axis=1), jnp.concatenate([-sin, sin], axis=1)


def kernel(x, p, w_in, sinks, w_branch_a, w_branch_b, w_out, ln1_g, ln1_b, peer_wq, peer_subkeys, peer_u,
           peer_v, ple_gate, ple_proj, ln2_g, ln2_b):
    bsz, seq, d = x.shape
    depth = w_in.shape[0]
    t = bsz * seq
    alpha = (2 * depth) ** 0.25
    assert seq % A_TQ == 0 and w_in.shape[2] == 3 * A_QKV + B_Q + 2 * B_KV + 2 * d
    cos2, sin2 = _rope_tables(seq)
    rope_cols = ((0, 2 * A_QKV), (3 * A_QKV, 3 * A_QKV + B_Q + B_KV))
    h = x.reshape(t, d)
    for i in range(depth):
        z = _inproj(h, w_in[i].astype(_BF), cos2, sin2, seq, rope_cols)
        z3 = z.reshape(bsz, seq, -1)
        oa = _attn_a(z3, bsz, seq).reshape(t, A_OUT)
        ob = _attn_b(z3, sinks[i].astype(_F32), bsz, seq).reshape(t, B_Q)
        h1, h1b, h1t = _merge(oa, ob, z, h, w_branch_a[i].astype(_BF), w_branch_b[i].astype(_BF),
                              w_out[i].astype(_BF), ln1_g[i].reshape(1, d), ln1_b[i].reshape(1, d), alpha)
        wq = peer_wq[i].astype(_BF).reshape(d, PEER_HEADS, 2 * PEER_HALF).transpose(1, 0, 2)
        p1, jj, e2, r2 = _route(h1b, wq, peer_subkeys[i].astype(_BF))
        ffn = _experts(h1t, peer_u[i].astype(_BF), peer_v[i].T.astype(_BF), p1, jj, e2, r2)
        h = _final(h1, h1b, ffn, p[i].reshape(t, -1).astype(_BF), ple_gate[i].astype(_BF),
                   ple_proj[i].astype(_BF), ln2_g[i].reshape(1, d), ln2_b[i].reshape(1, d), alpha)
    return h.reshape(bsz, seq, d)
```

```python
import functools

import numpy as np
import jax
import jax.numpy as jnp
from jax import lax
from jax.experimental import pallas as pl
from jax.experimental.pallas import tpu as pltpu

HEAD_DIM = 128
ROPE_THETA = 10000.0
BLOCK = 128
A_PATTERNS = ((128, 1), (512, 4), (2048, 16))
A_HEADS_PER_GROUP = 4
A_GROUPS = len(A_PATTERNS)
A_HEADS = A_HEADS_PER_GROUP * A_GROUPS
A_QKV = A_HEADS * HEAD_DIM
A_OUT = A_HEADS_PER_GROUP * HEAD_DIM
B_Q_HEADS = 8
B_KV_HEADS = 2
B_GROUP = B_Q_HEADS // B_KV_HEADS
B_Q = B_Q_HEADS * HEAD_DIM
B_KV = B_KV_HEADS * HEAD_DIM
PEER_HEADS = 8
PEER_HALF = 128
N_KEYS = 128
PEER_TOPK = 16
LN_EPS = 1e-5

V7X_VMEM_LIMIT = 60 * 1024 * 1024
BF16_ROWS = 16
A_TQ = max(d for _, d in A_PATTERNS) * BLOCK

ATTN_UNROLL = 8
B_UNROLL = 1

_NT = (((1,), (1,)), ((), ()))
_BF = jnp.bfloat16
_F32 = jnp.float32


def _cparams(sem):
    return pltpu.CompilerParams(dimension_semantics=sem, vmem_limit_bytes=V7X_VMEM_LIMIT)


def _resident(block_shape, index_map):
    return pl.BlockSpec(block_shape, index_map, pipeline_mode=pl.Buffered(1))


INPROJ_ROWS = 128


def _inproj_kernel(x_ref, w_ref, cos_ref, sin_ref, o_ref, xb_s, *, patterns):
    j = pl.program_id(1)
    tm = x_ref.shape[0]

    @pl.when(j == 0)
    def _():
        def cast(r, c):
            rows = pl.ds(pl.multiple_of(r * BLOCK, BLOCK), BLOCK)
            xb_s[rows, :] = x_ref[rows, :].astype(_BF)
            return c
        lax.fori_loop(0, tm // BLOCK, cast, 0)

    def matmul(r):
        rows = slice(r * INPROJ_ROWS, (r + 1) * INPROJ_ROWS)
        return jnp.dot(xb_s[rows, :], w_ref[...], preferred_element_type=_F32)

    def epilogue(r, acc):
        rows = slice(r * INPROJ_ROWS, (r + 1) * INPROJ_ROWS)
        for c in range(acc.shape[1] // HEAD_DIM):
            t = acc[:, c * HEAD_DIM:(c + 1) * HEAD_DIM]
            blocks = [jb for jb, pat in enumerate(patterns) if pat[c]]
            if blocks:
                flag = functools.reduce(jnp.logical_or, [j == jb for jb in blocks])
                rope = t * cos_ref[rows, :] + pltpu.roll(t, HEAD_DIM // 2, 1) * sin_ref[rows, :]
                t = jnp.where(flag, rope, t)
            o_ref[rows, c * HEAD_DIM:(c + 1) * HEAD_DIM] = t

    nr = tm // INPROJ_ROWS
    acc = matmul(0)
    for r in range(nr):
        nxt = matmul(r + 1) if r + 1 < nr else None
        epilogue(r, acc)
        acc = nxt


def _inproj(x, wb, cos2, sin2, seq, rope_cols, tm=1024, tn=1024):
    m, k = x.shape
    n = wb.shape[1]
    tm = min(tm, seq)
    tn = tn if n % tn == 0 else tn // 2
    assert n % tn == 0 and m % tm == 0
    patterns = tuple(
        tuple(any(lo <= (jb * tn + c * HEAD_DIM) < hi for lo, hi in rope_cols) for c in range(tn // HEAD_DIM))
        for jb in range(n // tn))
    sblocks = seq // tm
    return pl.pallas_call(
        functools.partial(_inproj_kernel, patterns=patterns),
        grid=(m // tm, n // tn),
        in_specs=[
            pl.BlockSpec((tm, k), lambda i, j: (i, 0)),
            pl.BlockSpec((k, tn), lambda i, j: (0, j)),
            pl.BlockSpec((tm, HEAD_DIM), lambda i, j: (i % sblocks, 0)),
            pl.BlockSpec((tm, HEAD_DIM), lambda i, j: (i % sblocks, 0)),
        ],
        out_specs=pl.BlockSpec((tm, tn), lambda i, j: (i, j)),
        out_shape=jax.ShapeDtypeStruct((m, n), _F32),
        scratch_shapes=[pltpu.VMEM((tm, k), _BF)],
        compiler_params=_cparams(("parallel", "arbitrary")),
        name="inproj_rope",
    )(x, wb, cos2, sin2)


def _band_scores(q, kc, kp, has_prev, strict_prev):
    scale = HEAD_DIM ** -0.5
    sc = lax.dot_general(q, kc, _NT, preferred_element_type=_F32) * scale
    sp = lax.dot_general(q, kp, _NT, preferred_element_type=_F32) * scale
    row = lax.broadcasted_iota(jnp.int32, sc.shape, 0) % BLOCK
    col = lax.broadcasted_iota(jnp.int32, sc.shape, 1)
    prev_ok = (col > row) if strict_prev else (col >= row)
    sc = jnp.where(col <= row, sc, -jnp.inf)
    sp = jnp.where(jnp.logical_and(prev_ok, has_prev), sp, -jnp.inf)
    return sc, sp


def _attn_a_kernel(q0, q1, q2, k0, k1, k2, v0, v1, v2, o_ref, acc_s, m_s, l_s):
    tq = o_ref.shape[0]
    t0 = pl.program_id(2) * tq
    refs = ((q0, k0, v0), (q1, k1, v1), (q2, k2, v2))
    for g, (window, dil) in enumerate(A_PATTERNS):
        q_ref, k_ref, v_ref = refs[g]
        span = BLOCK * dil

        def units(ug, carry, q_ref=q_ref, k_ref=k_ref, v_ref=v_ref, dil=dil, span=span, g=g):
            us = [ug * ATTN_UNROLL + i for i in range(ATTN_UNROLL)]
            locs = [(u // dil) * span + u % dil for u in us]
            scores = []
            for u, loc in zip(us, locs):
                has_prev = (t0 + (u // dil) * span) > 0
                pstart = jnp.maximum(t0 + loc - span, u % dil)
                q = q_ref[pl.ds(loc, BLOCK, stride=dil), :].astype(_BF)
                kc = k_ref[pl.ds(t0 + loc, BLOCK, stride=dil), :].astype(_BF)
                kp = k_ref[pl.ds(pstart, BLOCK, stride=dil), :].astype(_BF)
                scores.append(_band_scores(q, kc, kp, has_prev, strict_prev=False) + (pstart,))
            ms = [jnp.max(jnp.maximum(sc, sp), axis=1, keepdims=True) for sc, sp, _ in scores]
            probs = [(jnp.exp(sc - m), jnp.exp(sp - m)) for (sc, sp, _), m in zip(scores, ms)]
            ls = [jnp.sum(pc + pp, axis=1, keepdims=True) for pc, pp in probs]
            for loc, (_, _, pstart), m, (pc, pp), l in zip(locs, scores, ms, probs, ls):
                vc = v_ref[pl.ds(t0 + loc, BLOCK, stride=dil), :].astype(_BF)
                vp = v_ref[pl.ds(pstart, BLOCK, stride=dil), :].astype(_BF)
                acc = (jnp.dot(pc.astype(_BF), vc, preferred_element_type=_F32)
                       + jnp.dot(pp.astype(_BF), vp, preferred_element_type=_F32))
                rows = pl.ds(loc, BLOCK, stride=dil)
                acc_s[g, rows, :] = acc
                m_s[g, rows, :] = jnp.broadcast_to(m, acc.shape)
                l_s[g, rows, :] = jnp.broadcast_to(l, acc.shape)
            return carry

        lax.fori_loop(0, tq // (BLOCK * ATTN_UNROLL), units, 0)

    chunk = 2 * BLOCK

    def combine(c, carry):
        rows = pl.ds(pl.multiple_of(c * chunk, chunk), chunk)
        ms = [m_s[g, rows, :] for g in range(A_GROUPS)]
        mx = functools.reduce(jnp.maximum, ms)
        ws = [jnp.exp(m - mx) for m in ms]
        num = sum(w * acc_s[g, rows, :] for g, w in enumerate(ws))
        den = sum(w * l_s[g, rows, :] for g, w in enumerate(ws))
        o_ref[rows, :] = (num / den).astype(o_ref.dtype)
        return carry

    lax.fori_loop(0, tq // chunk, combine, 0)


def _attn_a(z, bsz, seq):
    tq = A_TQ
    nh = A_HEADS_PER_GROUP
    kcol = A_QKV // HEAD_DIM
    vcol = 2 * A_QKV // HEAD_DIM

    def qspec(g):
        return pl.BlockSpec((None, tq, HEAD_DIM), lambda b, j, t, g=g: (b, t, g * nh + j))

    def kvspec(g, base):
        return pl.BlockSpec((None, seq, HEAD_DIM), lambda b, j, t, g=g, base=base: (b, 0, base + g * nh + j))

    in_specs = ([qspec(g) for g in range(A_GROUPS)] + [kvspec(g, kcol) for g in range(A_GROUPS)]
                + [kvspec(g, vcol) for g in range(A_GROUPS)])
    return pl.pallas_call(
        _attn_a_kernel,
        grid=(bsz, nh, seq // tq),
        in_specs=in_specs,
        out_specs=pl.BlockSpec((None, tq, HEAD_DIM), lambda b, j, t: (b, t, j)),
        out_shape=jax.ShapeDtypeStruct((bsz, seq, A_OUT), _BF),
        scratch_shapes=[pltpu.VMEM((A_GROUPS, tq, HEAD_DIM), _F32)] * 3,
        compiler_params=_cparams(("parallel", "parallel", "arbitrary")),
        name="attn_dilated",
    )(*([z] * 9))


def _attn_b_kernel(sink_ref, q_ref, k_ref, v_ref, o_ref):
    tq = o_ref.shape[0]
    t0 = pl.program_id(2) * tq
    c = pl.program_id(1)
    sink = jnp.concatenate(
        [jnp.full((BLOCK, 1), sink_ref[c * B_GROUP + g], _F32) for g in range(B_GROUP)], axis=0)

    def units(ug, carry):
        locs = [pl.multiple_of((ug * B_UNROLL + i) * BLOCK, BLOCK) for i in range(B_UNROLL)]
        scores = []
        for loc in locs:
            has_prev = (t0 + loc) > 0
            pstart = pl.multiple_of(jnp.maximum(t0 + loc - BLOCK, 0), BLOCK)
            q4 = q_ref[pl.ds(loc, BLOCK), :].astype(_BF)
            q = jnp.concatenate([q4[:, g * HEAD_DIM:(g + 1) * HEAD_DIM] for g in range(B_GROUP)], axis=0)
            kc = k_ref[pl.ds(pl.multiple_of(t0 + loc, BLOCK), BLOCK), :].astype(_BF)
            kp = k_ref[pl.ds(pstart, BLOCK), :].astype(_BF)
            scores.append(_band_scores(q, kc, kp, has_prev, strict_prev=True) + (pstart,))
        ms = [jnp.maximum(jnp.max(jnp.maximum(sc, sp), axis=1, keepdims=True), sink) for sc, sp, _ in scores]
        probs = [(jnp.exp(sc - m), jnp.exp(sp - m)) for (sc, sp, _), m in zip(scores, ms)]
        denoms = [jnp.sum(pc + pp, axis=1, keepdims=True) + jnp.exp(sink - m) for (pc, pp), m in zip(probs, ms)]
        for loc, (_, _, pstart), (pc, pp), denom in zip(locs, scores, probs, denoms):
            vc = v_ref[pl.ds(pl.multiple_of(t0 + loc, BLOCK), BLOCK), :].astype(_BF)
            vp = v_ref[pl.ds(pstart, BLOCK), :].astype(_BF)
            acc = (jnp.dot(pc.astype(_BF), vc, preferred_element_type=_F32)
                   + jnp.dot(pp.astype(_BF), vp, preferred_element_type=_F32))
            o = (acc / denom).astype(o_ref.dtype)
            for g in range(B_GROUP):
                o_ref[pl.ds(loc, BLOCK), g * HEAD_DIM:(g + 1) * HEAD_DIM] = o[g * BLOCK:(g + 1) * BLOCK, :]
        return carry

    lax.fori_loop(0, tq // (BLOCK * B_UNROLL), units, 0)


def _attn_b(z, sinks, bsz, seq, tq=1024):
    tq = min(tq, seq)
    gq = B_GROUP * HEAD_DIM
    qcol = 3 * A_QKV // gq
    kcol = (3 * A_QKV + B_Q) // HEAD_DIM
    vcol = (3 * A_QKV + B_Q + B_KV) // HEAD_DIM
    return pl.pallas_call(
        _attn_b_kernel,
        grid=(bsz, B_KV_HEADS, seq // tq),
        in_specs=[
            pl.BlockSpec(memory_space=pltpu.SMEM),
            pl.BlockSpec((None, tq, gq), lambda b, c, t: (b, t, qcol + c)),
            pl.BlockSpec((None, seq, HEAD_DIM), lambda b, c, t: (b, 0, kcol + c)),
            pl.BlockSpec((None, seq, HEAD_DIM), lambda b, c, t: (b, 0, vcol + c)),
        ],
        out_specs=pl.BlockSpec((None, tq, gq), lambda b, c, t: (b, t, c)),
        out_shape=jax.ShapeDtypeStruct((bsz, seq, B_Q), _BF),
        compiler_params=_cparams(("parallel", "parallel", "arbitrary")),
        name="attn_swa_gqa",
    )(sinks, z, z, z)


ROW_GROUP = 128


def _layer_norm(h, g, b):
    mu = jnp.mean(h, axis=-1, keepdims=True)
    var = jnp.mean(jnp.square(h - mu), axis=-1, keepdims=True)
    return (h - mu) * lax.rsqrt(var + LN_EPS) * g + b


def _merge_kernel(oa_ref, ob_ref, ga_ref, gb_ref, x_ref, wa_ref, wb_ref, wo_ref, g_ref, b_ref,
                  h_ref, hb_ref, ht_ref, *, alpha):
    groups = [slice(r, r + ROW_GROUP) for r in range(0, x_ref.shape[0], ROW_GROUP)]

    def branches(rs):
        return (jnp.dot(oa_ref[rs, :], wa_ref[...], preferred_element_type=_F32),
                jnp.dot(ob_ref[rs, :], wb_ref[...], preferred_element_type=_F32))

    def mix(rs, ma, mb):
        merged = jax.nn.sigmoid(ga_ref[rs, :]) * ma + jax.nn.sigmoid(gb_ref[rs, :]) * mb
        return jnp.dot(merged.astype(_BF), wo_ref[...], preferred_element_type=_F32)

    def finish(rs, mixed):
        h = _layer_norm(alpha * x_ref[rs, :] + mixed, g_ref[...], b_ref[...])
        h_ref[rs, :] = h
        hb_ref[rs, :] = h.astype(_BF)
        ht_ref[:, rs] = h.T.astype(_BF)

    first = [branches(rs) for rs in groups]
    pending = None
    for rs, (ma, mb) in zip(groups, first):
        mixed = mix(rs, ma, mb)
        if pending is not None:
            finish(*pending)
        pending = (rs, mixed)
    finish(*pending)


def _merge(oa, ob, z2, x2, wa, wb, wo, g, b, alpha, tm=256):
    t, d = x2.shape
    gcol = (3 * A_QKV + B_Q + 2 * B_KV) // d
    return pl.pallas_call(
        functools.partial(_merge_kernel, alpha=alpha),
        grid=(t // tm,),
        in_specs=[
            pl.BlockSpec((tm, A_OUT), lambda i: (i, 0)),
            pl.BlockSpec((tm, B_Q), lambda i: (i, 0)),
            pl.BlockSpec((tm, d), lambda i: (i, gcol)),
            pl.BlockSpec((tm, d), lambda i: (i, gcol + 1)),
            pl.BlockSpec((tm, d), lambda i: (i, 0)),
            _resident((A_OUT, d), lambda i: (0, 0)),
            _resident((B_Q, d), lambda i: (0, 0)),
            _resident((d, d), lambda i: (0, 0)),
            _resident((1, d), lambda i: (0, 0)),
            _resident((1, d), lambda i: (0, 0)),
        ],
        out_specs=[pl.BlockSpec((tm, d), lambda i: (i, 0)), pl.BlockSpec((tm, d), lambda i: (i, 0)),
                   pl.BlockSpec((d, tm), lambda i: (0, i))],
        out_shape=[jax.ShapeDtypeStruct((t, d), _F32), jax.ShapeDtypeStruct((t, d), _BF),
                   jax.ShapeDtypeStruct((d, t), _BF)],
        compiler_params=_cparams(("parallel",)),
        name="merge_outproj_ln",
    )(oa, ob, z2, z2, x2, wa, wb, wo, g, b)


ROUTE_LANES = 256


def _extract_top(s, k):
    n = s.shape[0]
    iota = lax.broadcasted_iota(jnp.int32, s.shape, 0)
    rank = jnp.full(s.shape, float(k), _F32)
    vals, idxs = [], []
    for kk in range(k):
        m = jnp.max(s, axis=0, keepdims=True)
        idx = jnp.min(jnp.where(s == m, iota, n), axis=0, keepdims=True)
        sel = iota == idx
        rank = jnp.where(sel, float(kk), rank)
        s = jnp.where(sel, -jnp.inf, s)
        vals.append(m)
        idxs.append(idx)
    return jnp.concatenate(vals, axis=0), jnp.concatenate(idxs, axis=0), rank


def _route_exact(s1, s2):
    kk = PEER_TOPK
    a1, _, r1 = _extract_top(s1, kk)
    a2, _, r2 = _extract_top(s2, kk)
    cand = jnp.concatenate([a1[i:i + 1, :] + a2 for i in range(kk)], axis=0)
    best, cidx, _ = _extract_top(cand, kk)
    ci = cidx // kk
    z = jnp.sum(jnp.exp(best - best[0:1, :]), axis=0, keepdims=True)
    jj = jnp.zeros(s1.shape, _F32)
    for i in range(kk):
        cnt = jnp.sum((ci == i).astype(_F32), axis=0, keepdims=True)
        jj = jnp.where(r1 == float(i), cnt, jj)
    return jnp.exp(s1 - a1[0:1, :]) / z, jj, jnp.exp(s2 - a2[0:1, :]), r2


_STRIP_MARK = 2.0 ** 100


def _strip_top(s, k):
    vals = []
    for kk in range(k):
        m = jnp.max(s, axis=0, keepdims=True)
        s = jnp.where(s == m, -_STRIP_MARK * (1.0 + kk / k), s)
        vals.append(m)
    removed = s <= -_STRIP_MARK
    rank = jnp.where(removed, jnp.floor((s * (-1.0 / _STRIP_MARK) - 1.0) * k + 0.5), float(k))
    return jnp.concatenate(vals, axis=0), removed, rank


_CAND_GROUPS = ((0, PEER_TOPK),) + tuple((i, PEER_TOPK // 2) for i in range(1, PEER_TOPK // 2))


def _route_fast(s1, s2):
    kk = PEER_TOPK
    a1, gone1, _ = _strip_top(s1, kk)
    a2, gone2, r2 = _strip_top(s2, kk)
    lone = len(_CAND_GROUPS)
    pieces = [a1[i:i + 1, :] + a2[0:w, :] for i, w in _CAND_GROUPS] + [a1[lone:kk, :] + a2[0:1, :]]
    cand = jnp.concatenate(pieces, axis=0)
    _, sel, _ = _strip_top(cand, kk)
    picked = sel.astype(_F32)
    z = jnp.sum(jnp.where(sel, jnp.exp(cand - cand[0:1, :]), 0.0), axis=0, keepdims=True)
    rows, off = [], 0
    for _, w in _CAND_GROUPS:
        rows.append(jnp.sum(picked[off:off + w, :], axis=0, keepdims=True))
        off += w
    jcnt = jnp.concatenate(rows + [picked[off:off + kk - lone, :]], axis=0)
    jj = jnp.zeros(s1.shape, _F32)
    for i in range(kk):
        jj = jnp.where(s1 == a1[i:i + 1, :], jcnt[i:i + 1, :], jj)
    n1 = jnp.sum(gone1.astype(_F32), axis=0, keepdims=True)
    n2 = jnp.sum(gone2.astype(_F32), axis=0, keepdims=True)
    nc = jnp.sum(picked, axis=0, keepdims=True)
    bad = jnp.maximum(jnp.maximum(jnp.abs(n1 - kk), jnp.abs(n2 - kk)), jnp.abs(nc - kk))
    return (jnp.exp(s1 - a1[0:1, :]) / z, jj, jnp.exp(s2 - a2[0:1, :]), r2), jnp.max(bad) > 0.0


def _route_kernel(h_ref, wq_ref, sk_ref, p1_ref, jj_ref, e2_ref, r2_ref, q_s):
    tm = h_ref.shape[0]

    def project(h):
        q = jnp.dot(h_ref[...], wq_ref[h], preferred_element_type=_F32)
        q_s[2 * h] = q[:, :PEER_HALF].astype(_BF)
        q_s[2 * h + 1] = q[:, PEER_HALF:].astype(_BF)

    def put(h, lanes, outs):
        p1, jj, e2, r2 = outs
        width = p1.shape[1]
        p1_ref[h, :, lanes] = p1
        jj_ref[h, :, lanes] = jj
        e2_ref[h, :, :, lanes] = e2.astype(_BF).reshape(N_KEYS // BF16_ROWS, BF16_ROWS, width)
        r2_ref[h, :, :, lanes] = r2.astype(_BF).reshape(N_KEYS // BF16_ROWS, BF16_ROWS, width)

    def head(h, carry):
        s1 = lax.dot_general(sk_ref[h, 0], q_s[2 * h], _NT, preferred_element_type=_F32)
        s2 = lax.dot_general(sk_ref[h, 1], q_s[2 * h + 1], _NT, preferred_element_type=_F32)
        project(jnp.minimum(h + 1, PEER_HEADS - 1))
        tie = None
        for c in range(tm // ROUTE_LANES):
            lanes = slice(c * ROUTE_LANES, (c + 1) * ROUTE_LANES)
            outs, t = _route_fast(s1[:, lanes], s2[:, lanes])
            put(h, lanes, outs)
            tie = t if tie is None else jnp.logical_or(tie, t)

        @pl.when(tie)
        def _():
            put(h, slice(0, tm), _route_exact(s1, s2))
        return carry

    project(0)
    lax.fori_loop(0, PEER_HEADS, head, 0)


def _route(hb, wq, sk, tm=512):
    t, d = hb.shape
    out = jax.ShapeDtypeStruct((PEER_HEADS, N_KEYS, t), _F32)
    ospec = pl.BlockSpec((PEER_HEADS, N_KEYS, tm), lambda i: (0, 0, i))
    outb = jax.ShapeDtypeStruct((PEER_HEADS, N_KEYS // BF16_ROWS, BF16_ROWS, t), _BF)
    obspec = pl.BlockSpec((PEER_HEADS, N_KEYS // BF16_ROWS, BF16_ROWS, tm), lambda i: (0, 0, 0, i))
    return pl.pallas_call(
        _route_kernel,
        grid=(t // tm,),
        in_specs=[
            pl.BlockSpec((tm, d), lambda i: (i, 0)),
            _resident(wq.shape, lambda i: (0, 0, 0)),
            _resident(sk.shape, lambda i: (0, 0, 0, 0)),
        ],
        out_specs=[ospec, ospec, obspec, obspec],
        out_shape=[out, out, outb, outb],
        scratch_shapes=[pltpu.VMEM((2 * PEER_HEADS, tm, PEER_HALF), _BF)],
        compiler_params=_cparams(("parallel",)),
        name="peer_route",
    )(hb, wq, sk)


EXPERT_SUB = 256


def _gform(b, p1_ref, jj_ref, e2_ref, r2_ref):
    tm = jj_ref.shape[2]
    gsum = None
    for h in range(PEER_HEADS):
        jrow = jnp.broadcast_to(jj_ref[h, b:b + 1, :], (BF16_ROWS, tm)).astype(_BF)[None]
        prow = jnp.broadcast_to(p1_ref[h, b:b + 1, :], (BF16_ROWS, tm)).astype(_BF)[None]
        term = jnp.where(r2_ref[h] < jrow, e2_ref[h], jnp.zeros((), _BF)) * prow
        gsum = term if gsum is None else gsum + term
    return gsum


def _experts_kernel(ht_ref, u_ref, vt_ref, p1_ref, jj_ref, e2_ref, r2_ref, o_ref, acc_s, gelu_s, w_s):
    e = pl.program_id(1)
    eb, tm = gelu_s.shape

    @pl.when(e == 0)
    def _():
        def zero(r, c):
            acc_s[pl.ds(pl.multiple_of(r * BLOCK, BLOCK), BLOCK), :] = jnp.zeros((BLOCK, tm), _F32)
            return c
        lax.fori_loop(0, acc_s.shape[0] // BLOCK, zero, 0)

    for a in range(eb // EXPERT_SUB):
        rows = slice(a * EXPERT_SUB, (a + 1) * EXPERT_SUB)
        x = jnp.dot(u_ref[rows, :], ht_ref[...], preferred_element_type=_F32)
        gelu_s[rows, :] = (0.5 * x * (1.0 + lax.erf(x * np.float32(np.sqrt(0.5))))).astype(_BF)
    for b in range(eb // N_KEYS):
        rows = slice(b * N_KEYS, (b + 1) * N_KEYS)
        g = _gform(b, p1_ref, jj_ref, e2_ref, r2_ref)
        w_s[rows, :] = (g * gelu_s[rows, :].reshape(N_KEYS // BF16_ROWS, BF16_ROWS, tm)).reshape(N_KEYS, tm)
    acc_s[...] += jnp.dot(vt_ref[...], w_s[...], preferred_element_type=_F32)

    @pl.when(e == pl.num_programs(1) - 1)
    def _():
        o_ref[...] = acc_s[...].T


def _experts(htb, ub, vtb, p1, jj, e2, r2, tm=512, eb=2048):
    d, t = htb.shape
    ne = ub.shape[0]
    rspec = pl.BlockSpec((PEER_HEADS, eb // N_KEYS, tm), lambda i, e: (0, e, i))
    bspec = pl.BlockSpec((PEER_HEADS, N_KEYS // BF16_ROWS, BF16_ROWS, tm), lambda i, e: (0, 0, 0, i),
                         pipeline_mode=pl.Buffered(1))
    return pl.pallas_call(
        _experts_kernel,
        grid=(t // tm, ne // eb),
        in_specs=[
            pl.BlockSpec((d, tm), lambda i, e: (0, i), pipeline_mode=pl.Buffered(1)),
            pl.BlockSpec((eb, d), lambda i, e: (e, 0)),
            pl.BlockSpec((d, eb), lambda i, e: (0, e)),
            rspec, rspec, bspec, bspec,
        ],
        out_specs=pl.BlockSpec((tm, d), lambda i, e: (i, 0)),
        out_shape=jax.ShapeDtypeStruct((t, d), _F32),
        scratch_shapes=[pltpu.VMEM((d, tm), _F32), pltpu.VMEM((eb, tm), _BF), pltpu.VMEM((eb, tm), _BF)],
        compiler_params=_cparams(("parallel", "arbitrary")),
        name="peer_experts",
    )(htb, ub, vtb, p1, jj, e2, r2)


def _final_kernel(h_ref, hb_ref, f_ref, p_ref, wg_ref, wp_ref, g_ref, b_ref, o_ref, *, alpha):
    groups = [slice(r, r + ROW_GROUP) for r in range(0, h_ref.shape[0], ROW_GROUP)]
    dots = [(jnp.dot(hb_ref[rs, :], wg_ref[...], preferred_element_type=_F32),
             jnp.dot(p_ref[rs, :], wp_ref[...], preferred_element_type=_F32)) for rs in groups]
    for rs, (gate, proj) in zip(groups, dots):
        ple = jax.nn.sigmoid(gate) * proj
        o_ref[rs, :] = _layer_norm(alpha * h_ref[rs, :] + f_ref[rs, :] + ple, g_ref[...], b_ref[...])


def _final(h, hb, ffn, pb, wg, wp, g, b, alpha, tm=256):
    t, d = h.shape
    pd = pb.shape[1]
    return pl.pallas_call(
        functools.partial(_final_kernel, alpha=alpha),
        grid=(t // tm,),
        in_specs=[
            pl.BlockSpec((tm, d), lambda i: (i, 0)),
            pl.BlockSpec((tm, d), lambda i: (i, 0)),
            pl.BlockSpec((tm, d), lambda i: (i, 0)),
            pl.BlockSpec((tm, pd), lambda i: (i, 0)),
            _resident((d, d), lambda i: (0, 0)),
            _resident((pd, d), lambda i: (0, 0)),
            _resident((1, d), lambda i: (0, 0)),
            _resident((1, d), lambda i: (0, 0)),
        ],
        out_specs=pl.BlockSpec((tm, d), lambda i: (i, 0)),
        out_shape=jax.ShapeDtypeStruct((t, d), _F32),
        compiler_params=_cparams(("parallel",)),
        name="ple_final_ln",
    )(h, hb, ffn, pb, wg, wp, g, b)


def _rope_tables(seq):
    inv = 1.0 / (ROPE_THETA ** (jnp.arange(0, HEAD_DIM, 2, dtype=_F32) / HEAD_DIM))
    ang = jnp.arange(seq, dtype=_F32)[:, None] * inv[None, :]
    cos, sin = jnp.cos(ang), jnp.sin(ang)
    return jnp.concatenate([cos, cos], axis=1), jnp.concatenate([-sin, sin], axis=1)


def kernel(x, p, w_in, sinks, w_branch_a, w_branch_b, w_out, ln1_g, ln1_b, peer_wq, peer_subkeys, peer_u,
           peer_v, ple_gate, ple_proj, ln2_g, ln2_b):
    bsz, seq, d = x.shape
    depth = w_in.shape[0]
    t = bsz * seq
    alpha = (2 * depth) ** 0.25
    assert seq % A_TQ == 0 and w_in.shape[2] == 3 * A_QKV + B_Q + 2 * B_KV + 2 * d
    cos2, sin2 = _rope_tables(seq)
    rope_cols = ((0, 2 * A_QKV), (3 * A_QKV, 3 * A_QKV + B_Q + B_KV))
    h = x.reshape(t, d)
    for i in range(depth):
        z = _inproj(h, w_in[i].astype(_BF), cos2, sin2, seq, rope_cols)
        z3 = z.reshape(bsz, seq, -1)
        oa = _attn_a(z3, bsz, seq).reshape(t, A_OUT)
        ob = _attn_b(z3, sinks[i].astype(_F32), bsz, seq).reshape(t, B_Q)
        h1, h1b, h1t = _merge(oa, ob, z, h, w_branch_a[i].astype(_BF), w_branch_b[i].astype(_BF),
                              w_out[i].astype(_BF), ln1_g[i].reshape(1, d), ln1_b[i].reshape(1, d), alpha)
        wq = peer_wq[i].astype(_BF).reshape(d, PEER_HEADS, 2 * PEER_HALF).transpose(1, 0, 2)
        p1, jj, e2, r2 = _route(h1b, wq, peer_subkeys[i].astype(_BF))
        ffn = _experts(h1t, peer_u[i].astype(_BF), peer_v[i].T.astype(_BF), p1, jj, e2, r2)
        h = _final(h1, h1b, ffn, p[i].reshape(t, -1).astype(_BF), ple_gate[i].astype(_BF),
                   ple_proj[i].astype(_BF), ln2_g[i].reshape(1, d), ln2_b[i].reshape(1, d), alpha)
    return h.reshape(bsz, seq, d)
```

```python
import functools

import numpy as np
import jax
import jax.numpy as jnp
from jax import lax
from jax.experimental import pallas as pl
from jax.experimental.pallas import tpu as pltpu

HEAD_DIM = 128
ROPE_THETA = 10000.0
BLOCK = 128
A_PATTERNS = ((128, 1), (512, 4), (2048, 16))
A_HEADS_PER_GROUP = 4
A_GROUPS = len(A_PATTERNS)
A_HEADS = A_HEADS_PER_GROUP * A_GROUPS
A_QKV = A_HEADS * HEAD_DIM
A_OUT = A_HEADS_PER_GROUP * HEAD_DIM
B_Q_HEADS = 8
B_KV_HEADS = 2
B_GROUP = B_Q_HEADS // B_KV_HEADS
B_Q = B_Q_HEADS * HEAD_DIM
B_KV = B_KV_HEADS * HEAD_DIM
PEER_HEADS = 8
PEER_HALF = 128
N_KEYS = 128
PEER_TOPK = 16
LN_EPS = 1e-5

V7X_VMEM_LIMIT = 60 * 1024 * 1024
BF16_ROWS = 16
A_TQ = max(d for _, d in A_PATTERNS) * BLOCK

ATTN_UNROLL = 16
B_UNROLL = 1

_NT = (((1,), (1,)), ((), ()))
_BF = jnp.bfloat16
_F32 = jnp.float32


def _cparams(sem):
    return pltpu.CompilerParams(dimension_semantics=sem, vmem_limit_bytes=V7X_VMEM_LIMIT)


def _resident(block_shape, index_map):
    return pl.BlockSpec(block_shape, index_map, pipeline_mode=pl.Buffered(1))


INPROJ_ROWS = 128


def _inproj_kernel(x_ref, w_ref, cos_ref, sin_ref, o_ref, xb_s, *, patterns):
    j = pl.program_id(1)
    tm = x_ref.shape[0]

    @pl.when(j == 0)
    def _():
        def cast(r, c):
            rows = pl.ds(pl.multiple_of(r * BLOCK, BLOCK), BLOCK)
            xb_s[rows, :] = x_ref[rows, :].astype(_BF)
            return c
        lax.fori_loop(0, tm // BLOCK, cast, 0)

    def matmul(r):
        rows = slice(r * INPROJ_ROWS, (r + 1) * INPROJ_ROWS)
        return jnp.dot(xb_s[rows, :], w_ref[...], preferred_element_type=_F32)

    def epilogue(r, acc):
        rows = slice(r * INPROJ_ROWS, (r + 1) * INPROJ_ROWS)
        for c in range(acc.shape[1] // HEAD_DIM):
            t = acc[:, c * HEAD_DIM:(c + 1) * HEAD_DIM]
            blocks = [jb for jb, pat in enumerate(patterns) if pat[c]]
            if blocks:
                flag = functools.reduce(jnp.logical_or, [j == jb for jb in blocks])
                rope = t * cos_ref[rows, :] + pltpu.roll(t, HEAD_DIM // 2, 1) * sin_ref[rows, :]
                t = jnp.where(flag, rope, t)
            o_ref[rows, c * HEAD_DIM:(c + 1) * HEAD_DIM] = t

    nr = tm // INPROJ_ROWS
    acc = matmul(0)
    for r in range(nr):
        nxt = matmul(r + 1) if r + 1 < nr else None
        epilogue(r, acc)
        acc = nxt


def _inproj(x, wb, cos2, sin2, seq, rope_cols, tm=1024, tn=1024):
    m, k = x.shape
    n = wb.shape[1]
    tm = min(tm, seq)
    tn = tn if n % tn == 0 else tn // 2
    assert n % tn == 0 and m % tm == 0
    patterns = tuple(
        tuple(any(lo <= (jb * tn + c * HEAD_DIM) < hi for lo, hi in rope_cols) for c in range(tn // HEAD_DIM))
        for jb in range(n // tn))
    sblocks = seq // tm
    return pl.pallas_call(
        functools.partial(_inproj_kernel, patterns=patterns),
        grid=(m // tm, n // tn),
        in_specs=[
            pl.BlockSpec((tm, k), lambda i, j: (i, 0)),
            pl.BlockSpec((k, tn), lambda i, j: (0, j)),
            pl.BlockSpec((tm, HEAD_DIM), lambda i, j: (i % sblocks, 0)),
            pl.BlockSpec((tm, HEAD_DIM), lambda i, j: (i % sblocks, 0)),
        ],
        out_specs=pl.BlockSpec((tm, tn), lambda i, j: (i, j)),
        out_shape=jax.ShapeDtypeStruct((m, n), _F32),
        scratch_shapes=[pltpu.VMEM((tm, k), _BF)],
        compiler_params=_cparams(("parallel", "arbitrary")),
        name="inproj_rope",
    )(x, wb, cos2, sin2)


def _band_scores(q, kc, kp, has_prev, strict_prev):
    scale = HEAD_DIM ** -0.5
    sc = lax.dot_general(q, kc, _NT, preferred_element_type=_F32) * scale
    sp = lax.dot_general(q, kp, _NT, preferred_element_type=_F32) * scale
    row = lax.broadcasted_iota(jnp.int32, sc.shape, 0) % BLOCK
    col = lax.broadcasted_iota(jnp.int32, sc.shape, 1)
    prev_ok = (col > row) if strict_prev else (col >= row)
    sc = jnp.where(col <= row, sc, -jnp.inf)
    sp = jnp.where(jnp.logical_and(prev_ok, has_prev), sp, -jnp.inf)
    return sc, sp


def _attn_a_kernel(q0, q1, q2, k0, k1, k2, v0, v1, v2, o_ref, acc_s, m_s, l_s):
    tq = o_ref.shape[0]
    t0 = pl.program_id(2) * tq
    refs = ((q0, k0, v0), (q1, k1, v1), (q2, k2, v2))
    for g, (window, dil) in enumerate(A_PATTERNS):
        q_ref, k_ref, v_ref = refs[g]
        span = BLOCK * dil

        def units(ug, carry, q_ref=q_ref, k_ref=k_ref, v_ref=v_ref, dil=dil, span=span, g=g):
            us = [ug * ATTN_UNROLL + i for i in range(ATTN_UNROLL)]
            locs = [(u // dil) * span + u % dil for u in us]
            scores = []
            for u, loc in zip(us, locs):
                has_prev = (t0 + (u // dil) * span) > 0
                pstart = jnp.maximum(t0 + loc - span, u % dil)
                q = q_ref[pl.ds(loc, BLOCK, stride=dil), :].astype(_BF)
                kc = k_ref[pl.ds(t0 + loc, BLOCK, stride=dil), :].astype(_BF)
                kp = k_ref[pl.ds(pstart, BLOCK, stride=dil), :].astype(_BF)
                scores.append(_band_scores(q, kc, kp, has_prev, strict_prev=False) + (pstart,))
            ms = [jnp.max(jnp.maximum(sc, sp), axis=1, keepdims=True) for sc, sp, _ in scores]
            probs = [(jnp.exp(sc - m), jnp.exp(sp - m)) for (sc, sp, _), m in zip(scores, ms)]
            ls = [jnp.sum(pc + pp, axis=1, keepdims=True) for pc, pp in probs]
            for loc, (_, _, pstart), m, (pc, pp), l in zip(locs, scores, ms, probs, ls):
                vc = v_ref[pl.ds(t0 + loc, BLOCK, stride=dil), :].astype(_BF)
                vp = v_ref[pl.ds(pstart, BLOCK, stride=dil), :].astype(_BF)
                acc = (jnp.dot(pc.astype(_BF), vc, preferred_element_type=_F32)
                       + jnp.dot(pp.astype(_BF), vp, preferred_element_type=_F32))
                rows = pl.ds(loc, BLOCK, stride=dil)
                acc_s[g, rows, :] = acc
                m_s[g, rows, :] = jnp.broadcast_to(m, acc.shape)
                l_s[g, rows, :] = jnp.broadcast_to(l, acc.shape)
            return carry

        lax.fori_loop(0, tq // (BLOCK * ATTN_UNROLL), units, 0)

    chunk = 2 * BLOCK

    def combine(c, carry):
        rows = pl.ds(pl.multiple_of(c * chunk, chunk), chunk)
        ms = [m_s[g, rows, :] for g in range(A_GROUPS)]
        mx = functools.reduce(jnp.maximum, ms)
        ws = [jnp.exp(m - mx) for m in ms]
        num = sum(w * acc_s[g, rows, :] for g, w in enumerate(ws))
        den = sum(w * l_s[g, rows, :] for g, w in enumerate(ws))
        o_ref[rows, :] = (num / den).astype(o_ref.dtype)
        return carry

    lax.fori_loop(0, tq // chunk, combine, 0)


def _attn_a(z, bsz, seq):
    tq = A_TQ
    nh = A_HEADS_PER_GROUP
    kcol = A_QKV // HEAD_DIM
    vcol = 2 * A_QKV // HEAD_DIM

    def qspec(g):
        return pl.BlockSpec((None, tq, HEAD_DIM), lambda b, j, t, g=g: (b, t, g * nh + j))

    def kvspec(g, base):
        return pl.BlockSpec((None, seq, HEAD_DIM), lambda b, j, t, g=g, base=base: (b, 0, base + g * nh + j))

    in_specs = ([qspec(g) for g in range(A_GROUPS)] + [kvspec(g, kcol) for g in range(A_GROUPS)]
                + [kvspec(g, vcol) for g in range(A_GROUPS)])
    return pl.pallas_call(
        _attn_a_kernel,
        grid=(bsz, nh, seq // tq),
        in_specs=in_specs,
        out_specs=pl.BlockSpec((None, tq, HEAD_DIM), lambda b, j, t: (b, t, j)),
        out_shape=jax.ShapeDtypeStruct((bsz, seq, A_OUT), _BF),
        scratch_shapes=[pltpu.VMEM((A_GROUPS, tq, HEAD_DIM), _F32)] * 3,
        compiler_params=_cparams(("parallel", "parallel", "arbitrary")),
        name="attn_dilated",
    )(*([z] * 9))


def _attn_b_kernel(sink_ref, q_ref, k_ref, v_ref, o_ref):
    tq = o_ref.shape[0]
    t0 = pl.program_id(2) * tq
    c = pl.program_id(1)
    sink = jnp.concatenate(
        [jnp.full((BLOCK, 1), sink_ref[c * B_GROUP + g], _F32) for g in range(B_GROUP)], axis=0)

    def units(ug, carry):
        locs = [pl.multiple_of((ug * B_UNROLL + i) * BLOCK, BLOCK) for i in range(B_UNROLL)]
        scores = []
        for loc in locs:
            has_prev = (t0 + loc) > 0
            pstart = pl.multiple_of(jnp.maximum(t0 + loc - BLOCK, 0), BLOCK)
            q4 = q_ref[pl.ds(loc, BLOCK), :].astype(_BF)
            q = jnp.concatenate([q4[:, g * HEAD_DIM:(g + 1) * HEAD_DIM] for g in range(B_GROUP)], axis=0)
            kc = k_ref[pl.ds(pl.multiple_of(t0 + loc, BLOCK), BLOCK), :].astype(_BF)
            kp = k_ref[pl.ds(pstart, BLOCK), :].astype(_BF)
            scores.append(_band_scores(q, kc, kp, has_prev, strict_prev=True) + (pstart,))
        ms = [jnp.maximum(jnp.max(jnp.maximum(sc, sp), axis=1, keepdims=True), sink) for sc, sp, _ in scores]
        probs = [(jnp.exp(sc - m), jnp.exp(sp - m)) for (sc, sp, _), m in zip(scores, ms)]
        denoms = [jnp.sum(pc + pp, axis=1, keepdims=True) + jnp.exp(sink - m) for (pc, pp), m in zip(probs, ms)]
        for loc, (_, _, pstart), (pc, pp), denom in zip(locs, scores, probs, denoms):
            vc = v_ref[pl.ds(pl.multiple_of(t0 + loc, BLOCK), BLOCK), :].astype(_BF)
            vp = v_ref[pl.ds(pstart, BLOCK), :].astype(_BF)
            acc = (jnp.dot(pc.astype(_BF), vc, preferred_element_type=_F32)
                   + jnp.dot(pp.astype(_BF), vp, preferred_element_type=_F32))
            o = (acc / denom).astype(o_ref.dtype)
            for g in range(B_GROUP):
                o_ref[pl.ds(loc, BLOCK), g * HEAD_DIM:(g + 1) * HEAD_DIM] = o[g * BLOCK:(g + 1) * BLOCK, :]
        return carry

    lax.fori_loop(0, tq // (BLOCK * B_UNROLL), units, 0)


def _attn_b(z, sinks, bsz, seq, tq=1024):
    tq = min(tq, seq)
    gq = B_GROUP * HEAD_DIM
    qcol = 3 * A_QKV // gq
    kcol = (3 * A_QKV + B_Q) // HEAD_DIM
    vcol = (3 * A_QKV + B_Q + B_KV) // HEAD_DIM
    return pl.pallas_call(
        _attn_b_kernel,
        grid=(bsz, B_KV_HEADS, seq // tq),
        in_specs=[
            pl.BlockSpec(memory_space=pltpu.SMEM),
            pl.BlockSpec((None, tq, gq), lambda b, c, t: (b, t, qcol + c)),
            pl.BlockSpec((None, seq, HEAD_DIM), lambda b, c, t: (b, 0, kcol + c)),
            pl.BlockSpec((None, seq, HEAD_DIM), lambda b, c, t: (b, 0, vcol + c)),
        ],
        out_specs=pl.BlockSpec((None, tq, gq), lambda b, c, t: (b, t, c)),
        out_shape=jax.ShapeDtypeStruct((bsz, seq, B_Q), _BF),
        compiler_params=_cparams(("parallel", "parallel", "arbitrary")),
        name="attn_swa_gqa",
    )(sinks, z, z, z)


ROW_GROUP = 128


def _layer_norm(h, g, b):
    mu = jnp.mean(h, axis=-1, keepdims=True)
    var = jnp.mean(jnp.square(h - mu), axis=-1, keepdims=True)
    return (h - mu) * lax.rsqrt(var + LN_EPS) * g + b


def _merge_kernel(oa_ref, ob_ref, ga_ref, gb_ref, x_ref, wa_ref, wb_ref, wo_ref, g_ref, b_ref,
                  h_ref, hb_ref, ht_ref, *, alpha):
    groups = [slice(r, r + ROW_GROUP) for r in range(0, x_ref.shape[0], ROW_GROUP)]

    def branches(rs):
        return (jnp.dot(oa_ref[rs, :], wa_ref[...], preferred_element_type=_F32),
                jnp.dot(ob_ref[rs, :], wb_ref[...], preferred_element_type=_F32))

    def mix(rs, ma, mb):
        merged = jax.nn.sigmoid(ga_ref[rs, :]) * ma + jax.nn.sigmoid(gb_ref[rs, :]) * mb
        return jnp.dot(merged.astype(_BF), wo_ref[...], preferred_element_type=_F32)

    def finish(rs, mixed):
        h = _layer_norm(alpha * x_ref[rs, :] + mixed, g_ref[...], b_ref[...])
        h_ref[rs, :] = h
        hb_ref[rs, :] = h.astype(_BF)
        ht_ref[:, rs] = h.T.astype(_BF)

    first = [branches(rs) for rs in groups]
    pending = None
    for rs, (ma, mb) in zip(groups, first):
        mixed = mix(rs, ma, mb)
        if pending is not None:
            finish(*pending)
        pending = (rs, mixed)
    finish(*pending)


def _merge(oa, ob, z2, x2, wa, wb, wo, g, b, alpha, tm=256):
    t, d = x2.shape
    gcol = (3 * A_QKV + B_Q + 2 * B_KV) // d
    return pl.pallas_call(
        functools.partial(_merge_kernel, alpha=alpha),
        grid=(t // tm,),
        in_specs=[
            pl.BlockSpec((tm, A_OUT), lambda i: (i, 0)),
            pl.BlockSpec((tm, B_Q), lambda i: (i, 0)),
            pl.BlockSpec((tm, d), lambda i: (i, gcol)),
            pl.BlockSpec((tm, d), lambda i: (i, gcol + 1)),
            pl.BlockSpec((tm, d), lambda i: (i, 0)),
            _resident((A_OUT, d), lambda i: (0, 0)),
            _resident((B_Q, d), lambda i: (0, 0)),
            _resident((d, d), lambda i: (0, 0)),
            _resident((1, d), lambda i: (0, 0)),
            _resident((1, d), lambda i: (0, 0)),
        ],
        out_specs=[pl.BlockSpec((tm, d), lambda i: (i, 0)), pl.BlockSpec((tm, d), lambda i: (i, 0)),
                   pl.BlockSpec((d, tm), lambda i: (0, i))],
        out_shape=[jax.ShapeDtypeStruct((t, d), _F32), jax.ShapeDtypeStruct((t, d), _BF),
                   jax.ShapeDtypeStruct((d, t), _BF)],
        compiler_params=_cparams(("parallel",)),
        name="merge_outproj_ln",
    )(oa, ob, z2, z2, x2, wa, wb, wo, g, b)


ROUTE_LANES = 256


def _extract_top(s, k):
    n = s.shape[0]
    iota = lax.broadcasted_iota(jnp.int32, s.shape, 0)
    rank = jnp.full(s.shape, float(k), _F32)
    vals, idxs = [], []
    for kk in range(k):
        m = jnp.max(s, axis=0, keepdims=True)
        idx = jnp.min(jnp.where(s == m, iota, n), axis=0, keepdims=True)
        sel = iota == idx
        rank = jnp.where(sel, float(kk), rank)
        s = jnp.where(sel, -jnp.inf, s)
        vals.append(m)
        idxs.append(idx)
    return jnp.concatenate(vals, axis=0), jnp.concatenate(idxs, axis=0), rank


def _route_exact(s1, s2):
    kk = PEER_TOPK
    a1, _, r1 = _extract_top(s1, kk)
    a2, _, r2 = _extract_top(s2, kk)
    cand = jnp.concatenate([a1[i:i + 1, :] + a2 for i in range(kk)], axis=0)
    best, cidx, _ = _extract_top(cand, kk)
    ci = cidx // kk
    z = jnp.sum(jnp.exp(best - best[0:1, :]), axis=0, keepdims=True)
    jj = jnp.zeros(s1.shape, _F32)
    for i in range(kk):
        cnt = jnp.sum((ci == i).astype(_F32), axis=0, keepdims=True)
        jj = jnp.where(r1 == float(i), cnt, jj)
    return jnp.exp(s1 - a1[0:1, :]) / z, jj, jnp.exp(s2 - a2[0:1, :]), r2


_STRIP_MARK = 2.0 ** 100


def _strip_top(s, k):
    vals = []
    for kk in range(k):
        m = jnp.max(s, axis=0, keepdims=True)
        s = jnp.where(s == m, -_STRIP_MARK * (1.0 + kk / k), s)
        vals.append(m)
    removed = s <= -_STRIP_MARK
    rank = jnp.where(removed, jnp.floor((s * (-1.0 / _STRIP_MARK) - 1.0) * k + 0.5), float(k))
    return jnp.concatenate(vals, axis=0), removed, rank


_CAND_GROUPS = ((0, PEER_TOPK),) + tuple((i, PEER_TOPK // 2) for i in range(1, PEER_TOPK // 2))


def _route_fast(s1, s2):
    kk = PEER_TOPK
    a1, gone1, _ = _strip_top(s1, kk)
    a2, gone2, r2 = _strip_top(s2, kk)
    lone = len(_CAND_GROUPS)
    pieces = [a1[i:i + 1, :] + a2[0:w, :] for i, w in _CAND_GROUPS] + [a1[lone:kk, :] + a2[0:1, :]]
    cand = jnp.concatenate(pieces, axis=0)
    _, sel, _ = _strip_top(cand, kk)
    picked = sel.astype(_F32)
    z = jnp.sum(jnp.where(sel, jnp.exp(cand - cand[0:1, :]), 0.0), axis=0, keepdims=True)
    rows, off = [], 0
    for _, w in _CAND_GROUPS:
        rows.append(jnp.sum(picked[off:off + w, :], axis=0, keepdims=True))
        off += w
    jcnt = jnp.concatenate(rows + [picked[off:off + kk - lone, :]], axis=0)
    jj = jnp.zeros(s1.shape, _F32)
    for i in range(kk):
        jj = jnp.where(s1 == a1[i:i + 1, :], jcnt[i:i + 1, :], jj)
    n1 = jnp.sum(gone1.astype(_F32), axis=0, keepdims=True)
    n2 = jnp.sum(gone2.astype(_F32), axis=0, keepdims=True)
    nc = jnp.sum(picked, axis=0, keepdims=True)
    bad = jnp.maximum(jnp.maximum(jnp.abs(n1 - kk), jnp.abs(n2 - kk)), jnp.abs(nc - kk))
    return (jnp.exp(s1 - a1[0:1, :]) / z, jj, jnp.exp(s2 - a2[0:1, :]), r2), jnp.max(bad) > 0.0


def _route_kernel(h_ref, wq_ref, sk_ref, p1_ref, jj_ref, e2_ref, r2_ref, q_s):
    tm = h_ref.shape[0]

    def project(h):
        q = jnp.dot(h_ref[...], wq_ref[h], preferred_element_type=_F32)
        q_s[2 * h] = q[:, :PEER_HALF].astype(_BF)
        q_s[2 * h + 1] = q[:, PEER_HALF:].astype(_BF)

    def put(h, lanes, outs):
        p1, jj, e2, r2 = outs
        width = p1.shape[1]
        p1_ref[h, :, lanes] = p1
        jj_ref[h, :, lanes] = jj
        e2_ref[h, :, :, lanes] = e2.astype(_BF).reshape(N_KEYS // BF16_ROWS, BF16_ROWS, width)
        r2_ref[h, :, :, lanes] = r2.astype(_BF).reshape(N_KEYS // BF16_ROWS, BF16_ROWS, width)

    def head(h, carry):
        s1 = lax.dot_general(sk_ref[h, 0], q_s[2 * h], _NT, preferred_element_type=_F32)
        s2 = lax.dot_general(sk_ref[h, 1], q_s[2 * h + 1], _NT, preferred_element_type=_F32)
        project(jnp.minimum(h + 1, PEER_HEADS - 1))
        tie = None
        for c in range(tm // ROUTE_LANES):
            lanes = slice(c * ROUTE_LANES, (c + 1) * ROUTE_LANES)
            outs, t = _route_fast(s1[:, lanes], s2[:, lanes])
            put(h, lanes, outs)
            tie = t if tie is None else jnp.logical_or(tie, t)

        @pl.when(tie)
        def _():
            put(h, slice(0, tm), _route_exact(s1, s2))
        return carry

    project(0)
    lax.fori_loop(0, PEER_HEADS, head, 0)


def _route(hb, wq, sk, tm=512):
    t, d = hb.shape
    out = jax.ShapeDtypeStruct((PEER_HEADS, N_KEYS, t), _F32)
    ospec = pl.BlockSpec((PEER_HEADS, N_KEYS, tm), lambda i: (0, 0, i))
    outb = jax.ShapeDtypeStruct((PEER_HEADS, N_KEYS // BF16_ROWS, BF16_ROWS, t), _BF)
    obspec = pl.BlockSpec((PEER_HEADS, N_KEYS // BF16_ROWS, BF16_ROWS, tm), lambda i: (0, 0, 0, i))
    return pl.pallas_call(
        _route_kernel,
        grid=(t // tm,),
        in_specs=[
            pl.BlockSpec((tm, d), lambda i: (i, 0)),
            _resident(wq.shape, lambda i: (0, 0, 0)),
            _resident(sk.shape, lambda i: (0, 0, 0, 0)),
        ],
        out_specs=[ospec, ospec, obspec, obspec],
        out_shape=[out, out, outb, outb],
        scratch_shapes=[pltpu.VMEM((2 * PEER_HEADS, tm, PEER_HALF), _BF)],
        compiler_params=_cparams(("parallel",)),
        name="peer_route",
    )(hb, wq, sk)


EXPERT_SUB = 256


def _gform(b, p1_ref, jj_ref, e2_ref, r2_ref):
    tm = jj_ref.shape[2]
    gsum = None
    for h in range(PEER_HEADS):
        jrow = jnp.broadcast_to(jj_ref[h, b:b + 1, :], (BF16_ROWS, tm)).astype(_BF)[None]
        prow = jnp.broadcast_to(p1_ref[h, b:b + 1, :], (BF16_ROWS, tm)).astype(_BF)[None]
        term = jnp.where(r2_ref[h] < jrow, e2_ref[h], jnp.zeros((), _BF)) * prow
        gsum = term if gsum is None else gsum + term
    return gsum


def _experts_kernel(ht_ref, u_ref, vt_ref, p1_ref, jj_ref, e2_ref, r2_ref, o_ref, acc_s, gelu_s, w_s):
    e = pl.program_id(1)
    eb, tm = gelu_s.shape

    @pl.when(e == 0)
    def _():
        def zero(r, c):
            acc_s[pl.ds(pl.multiple_of(r * BLOCK, BLOCK), BLOCK), :] = jnp.zeros((BLOCK, tm), _F32)
            return c
        lax.fori_loop(0, acc_s.shape[0] // BLOCK, zero, 0)

    for a in range(eb // EXPERT_SUB):
        rows = slice(a * EXPERT_SUB, (a + 1) * EXPERT_SUB)
        x = jnp.dot(u_ref[rows, :], ht_ref[...], preferred_element_type=_F32)
        gelu_s[rows, :] = (0.5 * x * (1.0 + lax.erf(x * np.float32(np.sqrt(0.5))))).astype(_BF)
    for b in range(eb // N_KEYS):
        rows = slice(b * N_KEYS, (b + 1) * N_KEYS)
        g = _gform(b, p1_ref, jj_ref, e2_ref, r2_ref)
        w_s[rows, :] = (g * gelu_s[rows, :].reshape(N_KEYS // BF16_ROWS, BF16_ROWS, tm)).reshape(N_KEYS, tm)
    acc_s[...] += jnp.dot(vt_ref[...], w_s[...], preferred_element_type=_F32)

    @pl.when(e == pl.num_programs(1) - 1)
    def _():
        o_ref[...] = acc_s[...].T


def _experts(htb, ub, vtb, p1, jj, e2, r2, tm=512, eb=2048):
    d, t = htb.shape
    ne = ub.shape[0]
    rspec = pl.BlockSpec((PEER_HEADS, eb // N_KEYS, tm), lambda i, e: (0, e, i))
    bspec = pl.BlockSpec((PEER_HEADS, N_KEYS // BF16_ROWS, BF16_ROWS, tm), lambda i, e: (0, 0, 0, i),
                         pipeline_mode=pl.Buffered(1))
    return pl.pallas_call(
        _experts_kernel,
        grid=(t // tm, ne // eb),
        in_specs=[
            pl.BlockSpec((d, tm), lambda i, e: (0, i), pipeline_mode=pl.Buffered(1)),
            pl.BlockSpec((eb, d), lambda i, e: (e, 0)),
            pl.BlockSpec((d, eb), lambda i, e: (0, e)),
            rspec, rspec, bspec, bspec,
        ],
        out_specs=pl.BlockSpec((tm, d), lambda i, e: (i, 0)),
        out_shape=jax.ShapeDtypeStruct((t, d), _F32),
        scratch_shapes=[pltpu.VMEM((d, tm), _F32), pltpu.VMEM((eb, tm), _BF), pltpu.VMEM((eb, tm), _BF)],
        compiler_params=_cparams(("parallel", "arbitrary")),
        name="peer_experts",
    )(htb, ub, vtb, p1, jj, e2, r2)


def _final_kernel(h_ref, hb_ref, f_ref, p_ref, wg_ref, wp_ref, g_ref, b_ref, o_ref, *, alpha):
    groups = [slice(r, r + ROW_GROUP) for r in range(0, h_ref.shape[0], ROW_GROUP)]
    dots = [(jnp.dot(hb_ref[rs, :], wg_ref[...], preferred_element_type=_F32),
             jnp.dot(p_ref[rs, :], wp_ref[...], preferred_element_type=_F32)) for rs in groups]
    for rs, (gate, proj) in zip(groups, dots):
        ple = jax.nn.sigmoid(gate) * proj
        o_ref[rs, :] = _layer_norm(alpha * h_ref[rs, :] + f_ref[rs, :] + ple, g_ref[...], b_ref[...])


def _final(h, hb, ffn, pb, wg, wp, g, b, alpha, tm=512):
    t, d = h.shape
    pd = pb.shape[1]
    return pl.pallas_call(
        functools.partial(_final_kernel, alpha=alpha),
        grid=(t // tm,),
        in_specs=[
            pl.BlockSpec((tm, d), lambda i: (i, 0)),
            pl.BlockSpec((tm, d), lambda i: (i, 0)),
            pl.BlockSpec((tm, d), lambda i: (i, 0)),
            pl.BlockSpec((tm, pd), lambda i: (i, 0)),
            _resident((d, d), lambda i: (0, 0)),
            _resident((pd, d), lambda i: (0, 0)),
            _resident((1, d), lambda i: (0, 0)),
            _resident((1, d), lambda i: (0, 0)),
        ],
        out_specs=pl.BlockSpec((tm, d), lambda i: (i, 0)),
        out_shape=jax.ShapeDtypeStruct((t, d), _F32),
        compiler_params=_cparams(("parallel",)),
        name="ple_final_ln",
    )(h, hb, ffn, pb, wg, wp, g, b)


def _rope_tables(seq):
    inv = 1.0 / (ROPE_THETA ** (jnp.arange(0, HEAD_DIM, 2, dtype=_F32) / HEAD_DIM))
    ang = jnp.arange(seq, dtype=_F32)[:, None] * inv[None, :]
    cos, sin = jnp.cos(ang), jnp.sin(ang)
    return jnp.concatenate([cos, cos], axis=1), jnp.concatenate([-sin, sin], axis=1)


def kernel(x, p, w_in, sinks, w_branch_a, w_branch_b, w_out, ln1_g, ln1_b, peer_wq, peer_subkeys, peer_u,
           peer_v, ple_gate, ple_proj, ln2_g, ln2_b):
    bsz, seq, d = x.shape
    depth = w_in.shape[0]
    t = bsz * seq
    alpha = (2 * depth) ** 0.25
    assert seq % A_TQ == 0 and w_in.shape[2] == 3 * A_QKV + B_Q + 2 * B_KV + 2 * d
    cos2, sin2 = _rope_tables(seq)
    rope_cols = ((0, 2 * A_QKV), (3 * A_QKV, 3 * A_QKV + B_Q + B_KV))
    h = x.reshape(t, d)
    for i in range(depth):
        z = _inproj(h, w_in[i].astype(_BF), cos2, sin2, seq, rope_cols)
        z3 = z.reshape(bsz, seq, -1)
        oa = _attn_a(z3, bsz, seq).reshape(t, A_OUT)
        ob = _attn_b(z3, sinks[i].astype(_F32), bsz, seq).reshape(t, B_Q)
        h1, h1b, h1t = _merge(oa, ob, z, h, w_branch_a[i].astype(_BF), w_branch_b[i].astype(_BF),
                              w_out[i].astype(_BF), ln1_g[i].reshape(1, d), ln1_b[i].reshape(1, d), alpha)
        wq = peer_wq[i].astype(_BF).reshape(d, PEER_HEADS, 2 * PEER_HALF).transpose(1, 0, 2)
        p1, jj, e2, r2 = _route(h1b, wq, peer_subkeys[i].astype(_BF))
        ffn = _experts(h1t, peer_u[i].astype(_BF), peer_v[i].T.astype(_BF), p1, jj, e2, r2)
        h = _final(h1, h1b, ffn, p[i].reshape(t, -1).astype(_BF), ple_gate[i].astype(_BF),
                   ple_proj[i].astype(_BF), ln2_g[i].reshape(1, d), ln2_b[i].reshape(1, d), alpha)
    return h.reshape(bsz, seq, d)
```
